```python
import math
import jax, jax.numpy as jnp
from jax import lax
import numpy as np

D_MODEL = 2048
BATCH = 4
SEQ = 4096
DEPTH = 2

D_MIX = D_MODEL
N_MIXERS = 4
MIX_W = D_MIX // N_MIXERS
IN_COLS = 6 * MIX_W
S5_GROUP_CH = 16
S5_GROUPS = MIX_W // S5_GROUP_CH
S5_STATE = 64
CONV_WIDTH = 31
LRU_HEADS = 8
LRU_HEAD_DIM = MIX_W // LRU_HEADS
LRU_CONV_WIDTH = 4
LRU_C = 8.0
POOL_WINDOWS = (2, 4, 8, 16)
POOL_GROUP_W = MIX_W // len(POOL_WINDOWS)
FFN_DIM = 5504
FFN_CONV_WIDTH = 3
EPS = 1e-6

kernel_name = "hybrid_parallel_s5_conformer_rglru_pool"


def rmsnorm(x, g):
    xf = x.astype(jnp.float32)
    y = xf * lax.rsqrt(jnp.mean(xf * xf, axis=-1, keepdims=True) + EPS)
    return (y * g.astype(jnp.float32)).astype(x.dtype)


def layernorm(x, g, b):
    xf = x.astype(jnp.float32)
    mu = jnp.mean(xf, axis=-1, keepdims=True)
    var = jnp.mean(jnp.square(xf - mu), axis=-1, keepdims=True)
    y = (xf - mu) * lax.rsqrt(var + EPS)
    return (y * g.astype(jnp.float32) + b.astype(jnp.float32)).astype(x.dtype)


def causal_dwconv(x, w, b):
    k, c = w.shape
    y = lax.conv_general_dilated(
        x, w[:, None, :].astype(x.dtype), window_strides=(1,), padding=((k - 1, 0),),
        dimension_numbers=('NWC', 'WIO', 'NWC'), feature_group_count=c)
    return y + b


def _linear_combine(left, right):
    a_l, b_l = left
    a_r, b_r = right
    return (a_l * a_r, a_r * b_l + b_r)


def s5_mixer(u, lam_re, lam_im, log_step, b_re, b_im, c_re, c_im, d, w_glu, b_glu):
    dt = u.dtype
    bsz, seqlen, _ = u.shape
    uf = u.astype(jnp.float32).reshape(bsz, seqlen, S5_GROUPS, S5_GROUP_CH)
    lam = lax.complex(lam_re.astype(jnp.float32), lam_im.astype(jnp.float32))
    step = jnp.exp(log_step.astype(jnp.float32))[:, None]
    lam_bar = jnp.exp(lam * step)
    bmat = lax.complex(b_re.astype(jnp.float32), b_im.astype(jnp.float32))
    b_bar = ((lam_bar - 1.0) / lam)[..., None] * bmat
    bu = jnp.einsum('blgh,gph->blgp', uf.astype(jnp.complex64), b_bar)
    a = jnp.broadcast_to(lam_bar, bu.shape)
    _, states = lax.associative_scan(_linear_combine, (a, bu), axis=1)
    cmat = lax.complex(c_re.astype(jnp.float32), c_im.astype(jnp.float32))
    y = jnp.einsum('blgp,ghp->blgh', states, cmat).real
    y = y + d.astype(jnp.float32).reshape(S5_GROUPS, S5_GROUP_CH) * uf
    y = y.reshape(bsz, seqlen, MIX_W)
    g = jax.nn.gelu(y, approximate=True)
    out = g * jax.nn.sigmoid(g @ w_glu.astype(jnp.float32) + b_glu.astype(jnp.float32))
    return out.astype(dt)


def conformer_conv_mixer(v, g, w_dw, b_dw, ln_g, ln_b, w_pw, b_pw):
    h = v * jax.nn.sigmoid(g)
    h = causal_dwconv(h, w_dw, b_dw)
    h = layernorm(h, ln_g, ln_b)
    h = jax.nn.silu(h)
    return h @ w_pw + b_pw


def rglru_mixer(xb, gb, w_conv, b_conv, w_r, b_r, w_i, b_i, lam):
    dt = xb.dtype
    bsz, seqlen, _ = xb.shape
    xc = causal_dwconv(xb, w_conv, b_conv)
    xh = xc.reshape(bsz, seqlen, LRU_HEADS, LRU_HEAD_DIM)
    r = jax.nn.sigmoid(jnp.einsum('blhd,hde->blhe', xh, w_r).reshape(bsz, seqlen, MIX_W) + b_r)
    i = jax.nn.sigmoid(jnp.einsum('blhd,hde->blhe', xh, w_i).reshape(bsz, seqlen, MIX_W) + b_i)
    log_a = -LRU_C * r.astype(jnp.float32) * jax.nn.softplus(-lam.astype(jnp.float32))
    a = jnp.exp(log_a)
    mult = jnp.sqrt(-jnp.expm1(2.0 * log_a))
    bt = mult * (i * xc).astype(jnp.float32)
    _, h = lax.associative_scan(_linear_combine, (a, bt), axis=1)
    return h.astype(dt) * jax.nn.gelu(gb, approximate=True)


def pool_mixer(xp, w, scale):
    dt = xp.dtype
    bsz, seqlen, _ = xp.shape
    xf = xp.astype(jnp.float32)
    cs = jnp.cumsum(xf, axis=1)
    cs_pad = jnp.concatenate([jnp.zeros((bsz, 1, MIX_W), jnp.float32), cs], axis=1)
    pos = jnp.arange(seqlen, dtype=jnp.float32) + 1.0
    diffs = []
    for gi, win in enumerate(POOL_WINDOWS):
        sl = slice(gi * POOL_GROUP_W, (gi + 1) * POOL_GROUP_W)
        upper = cs_pad[:, 1:, sl]
        lower = jnp.concatenate(
            [jnp.zeros((bsz, win - 1, POOL_GROUP_W), jnp.float32), cs_pad[:, :seqlen - win + 1, sl]], axis=1)
        count = jnp.minimum(pos, float(win))[None, :, None]
        diffs.append((upper - lower) / count - xf[:, :, sl])
    dg = jnp.stack(diffs, axis=2)
    y = jnp.einsum('blgc,gce->blge', dg, w.astype(jnp.float32)).reshape(bsz, seqlen, MIX_W)
    return (y * scale.astype(jnp.float32)).astype(dt)


def conv_gated_mlp(h, w_up, w_dw, b_dw, w_down):
    up = h @ w_up
    gate, val = jnp.split(up, 2, axis=-1)
    gate = causal_dwconv(gate, w_dw, b_dw)
    return (jax.nn.gelu(gate, approximate=True) * val) @ w_down


def setup_inputs(seed: int = 0) -> dict:
    key = jax.random.key(seed)
    ks = iter(jax.random.split(key, 40))
    nrm = lambda shape, std: std * jax.random.normal(next(ks), shape, jnp.float32)
    L_, G, P, H = DEPTH, S5_GROUPS, S5_STATE, S5_GROUP_CH
    x = jax.random.normal(next(ks), (BATCH, SEQ, D_MODEL), jnp.float32)
    lam_im_base = jnp.pi * jnp.arange(P, dtype=jnp.float32)
    a0 = jax.random.uniform(next(ks), (L_, MIX_W), jnp.float32, 0.9, 0.999)
    a_base = a0 ** (1.0 / LRU_C)
    return {
        "x": x,
        "norm_mix_g": 1.0 + nrm((L_, D_MODEL), 0.02),
        "w_in": nrm((L_, D_MODEL, IN_COLS), D_MODEL ** -0.5),
        "s5_lam_re": -0.5 + nrm((L_, G, P), 0.01),
        "s5_lam_im": lam_im_base + nrm((L_, G, P), 0.01),
        "s5_log_step": jax.random.uniform(next(ks), (L_, G), jnp.float32, math.log(0.001), math.log(0.1)),
        "s5_b_re": nrm((L_, G, P, H), (2.0 * H) ** -0.5),
        "s5_b_im": nrm((L_, G, P, H), (2.0 * H) ** -0.5),
        "s5_c_re": nrm((L_, G, H, P), (2.0 * P) ** -0.5),
        "s5_c_im": nrm((L_, G, H, P), (2.0 * P) ** -0.5),
        "s5_d": nrm((L_, MIX_W), 1.0),
        "s5_w_glu": nrm((L_, MIX_W, MIX_W), MIX_W ** -0.5),
        "s5_b_glu": nrm((L_, MIX_W), 0.01),
        "cv_w_dw": nrm((L_, CONV_WIDTH, MIX_W), CONV_WIDTH ** -0.5),
        "cv_b_dw": nrm((L_, MIX_W), 0.01),
        "cv_ln_g": 1.0 + nrm((L_, MIX_W), 0.02),
        "cv_ln_b": nrm((L_, MIX_W), 0.01),
        "cv_w_pw": nrm((L_, MIX_W, MIX_W), MIX_W ** -0.5),
        "cv_b_pw": nrm((L_, MIX_W), 0.01),
        "lru_w_conv": nrm((L_, LRU_CONV_WIDTH, MIX_W), LRU_CONV_WIDTH ** -0.5),
        "lru_b_conv": nrm((L_, MIX_W), 0.01),
        "lru_w_r": nrm((L_, LRU_HEADS, LRU_HEAD_DIM, LRU_HEAD_DIM), LRU_HEAD_DIM ** -0.5),
        "lru_b_r": nrm((L_, MIX_W), 0.01),
        "lru_w_i": nrm((L_, LRU_HEADS, LRU_HEAD_DIM, LRU_HEAD_DIM), LRU_HEAD_DIM ** -0.5),
        "lru_b_i": nrm((L_, MIX_W), 0.01),
        "lru_lam": jnp.log(a_base) - jnp.log1p(-a_base),
        "pool_w": nrm((L_, len(POOL_WINDOWS), POOL_GROUP_W, POOL_GROUP_W), POOL_GROUP_W ** -0.5),
        "pool_scale": 1.0 + nrm((L_, MIX_W), 0.02),
        "w_out": nrm((L_, D_MIX, D_MODEL), D_MIX ** -0.5),
        "norm_ffn_g": 1.0 + nrm((L_, D_MODEL), 0.02),
        "ffn_w_up": nrm((L_, D_MODEL, 2 * FFN_DIM), D_MODEL ** -0.5),
        "ffn_w_dw": nrm((L_, FFN_CONV_WIDTH, FFN_DIM), FFN_CONV_WIDTH ** -0.5),
        "ffn_b_dw": nrm((L_, FFN_DIM), 0.01),
        "ffn_w_down": nrm((L_, FFN_DIM, D_MODEL), FFN_DIM ** -0.5),
        "norm_final_g": 1.0 + nrm((D_MODEL,), 0.02),
    }


def reference(x, norm_mix_g, w_in, s5_lam_re, s5_lam_im, s5_log_step, s5_b_re, s5_b_im,
              s5_c_re, s5_c_im, s5_d, s5_w_glu, s5_b_glu, cv_w_dw, cv_b_dw, cv_ln_g, cv_ln_b,
              cv_w_pw, cv_b_pw, lru_w_conv, lru_b_conv, lru_w_r, lru_b_r, lru_w_i, lru_b_i,
              lru_lam, pool_w, pool_scale, w_out, norm_ffn_g, ffn_w_up, ffn_w_dw, ffn_b_dw,
              ffn_w_down, norm_final_g):
    split_idx = [MIX_W * k for k in range(1, 6)]
    for l in range(DEPTH):
        h = rmsnorm(x, norm_mix_g[l])
        proj = h @ w_in[l]
        s5_u, cv_v, cv_g, lru_x, lru_g, pool_x = jnp.split(proj, split_idx, axis=-1)
        y_s5 = s5_mixer(s5_u, s5_lam_re[l], s5_lam_im[l], s5_log_step[l], s5_b_re[l], s5_b_im[l],
                        s5_c_re[l], s5_c_im[l], s5_d[l], s5_w_glu[l], s5_b_glu[l])
        y_cv = conformer_conv_mixer(cv_v, cv_g, cv_w_dw[l], cv_b_dw[l], cv_ln_g[l], cv_ln_b[l],
                                    cv_w_pw[l], cv_b_pw[l])
        y_lru = rglru_mixer(lru_x, lru_g, lru_w_conv[l], lru_b_conv[l], lru_w_r[l], lru_b_r[l],
                            lru_w_i[l], lru_b_i[l], lru_lam[l])
        y_pool = pool_mixer(pool_x, pool_w[l], pool_scale[l])
        mixed = jnp.concatenate([y_s5, y_cv, y_lru, y_pool], axis=-1)
        x = x + mixed @ w_out[l]
        h = rmsnorm(x, norm_ffn_g[l])
        x = x + conv_gated_mlp(h, ffn_w_up[l], ffn_w_dw[l], ffn_b_dw[l], ffn_w_down[l])
    return rmsnorm(x, norm_final_g)
```

```python
import functools

import jax
import jax.numpy as jnp
from jax import lax
from jax.experimental import pallas as pl
from jax.experimental.pallas import tpu as pltpu

D_MODEL = 2048
MIX_W = 512
IN_COLS = 6 * MIX_W
S5_GROUP_CH = 16
S5_STATE = 64
CONV_WIDTH = 31
LRU_HEADS = 8
LRU_HEAD_DIM = MIX_W // LRU_HEADS
LRU_CONV_WIDTH = 4
LRU_C = 8.0
POOL_WINDOWS = (2, 4, 8, 16)
FFN_DIM = 5504
FFN_CONV_WIDTH = 3
EPS = 1e-6

SUBLANES = 8
LANES = 128

T = 256
KSUB = T // SUBLANES
FFN_TM = 1024
FFN_TF = 512
FFN_PAD = 5632
N_FCHUNK = FFN_PAD // FFN_TF
S5_BLK = 4
S5_BLK_STATES = 512
VMEM_LIMIT = 56 * 1024 * 1024

F32 = jnp.float32
BF16 = jnp.bfloat16


def _dot(a, b):
    return jnp.dot(a, b, preferred_element_type=F32)


def _rms(x, g):
    ms = jnp.mean(x * x, axis=-1, keepdims=True)
    return x * lax.rsqrt(ms + EPS) * g


def _cmul(ar, ai, br, bi):
    return ar * br - ai * bi, ar * bi + ai * br


def _sub_roll(x, d):
    return pltpu.roll(x, d, 0)


def _fill_ext(ext_ref, tail_ref, a, dv):
    w = a.shape[1]
    n = SUBLANES * dv
    ext_ref[n:n + T, :] = a
    tail = a[T - n:, :].reshape(dv, SUBLANES, w)
    prev = tail_ref[...].reshape(dv, SUBLANES, w)
    sub = lax.broadcasted_iota(jnp.int32, (dv, SUBLANES, w), 1)
    halo = pltpu.roll(jnp.where(sub == SUBLANES - 1, prev, tail), 1, 1)
    ext_ref[0:n, :] = halo.reshape(n, w)
    tail_ref[...] = a[T - n:, :]


def _mix_kernel(x_ref, g_ref, win_ref, bw_ref, apow_ref, cw_ref, d_ref, wglu_ref, bglu_ref,
                cvw_ref, cvb_ref, lng_ref, lnb_ref, wpw_ref, bpw_ref,
                lw_ref, lb_ref, wr_ref, br_ref, wi_ref, bi_ref, sp_ref,
                pw_ref, ps_ref, wout_ref, o_ref,
                proj_s, st_s, scar_s, cvext_s, cvtail_s, tmp_s,
                lext_s, ltail_s, la_s, lb_s, lcar_s, pext_s, ptail_s, mixed_s):
    c = pl.program_id(1)

    @pl.when(c == 0)
    def _():
        scar_s[...] = jnp.zeros_like(scar_s)
        cvtail_s[...] = jnp.zeros_like(cvtail_s)
        ltail_s[...] = jnp.zeros_like(ltail_s)
        lcar_s[...] = jnp.zeros_like(lcar_s)
        ptail_s[...] = jnp.zeros_like(ptail_s)

    h = _rms(x_ref[...], g_ref[...]).astype(BF16)
    proj_s[...] = _dot(h, win_ref[...])

    sub512 = lax.broadcasted_iota(jnp.int32, (SUBLANES, MIX_W), 0)

    for m in range(S5_BLK):
        u = proj_s[:, LANES * m:LANES * (m + 1)]
        st_s[:, 1024 * m:1024 * (m + 1)] = _dot(u.astype(BF16), bw_ref[m])

    for m in range(S5_BLK):
        rc = slice(1024 * m, 1024 * m + S5_BLK_STATES)
        ic = slice(1024 * m + S5_BLK_STATES, 1024 * (m + 1))
        a_re = jnp.broadcast_to(apow_ref[0:1, rc], (SUBLANES, S5_BLK_STATES))
        a_im = jnp.broadcast_to(apow_ref[0:1, ic], (SUBLANES, S5_BLK_STATES))
        s0r = jnp.where(sub512 == 0, _sub_roll(scar_s[:, rc], 1), 0.0)
        s0i = jnp.where(sub512 == 0, _sub_roll(scar_s[:, ic], 1), 0.0)

        def scan_body(k, carry, rc=rc, ic=ic, a_re=a_re, a_im=a_im):
            sr, si = carry
            r0 = pl.multiple_of(k * SUBLANES, SUBLANES)
            tr, ti = _cmul(a_re, a_im, sr, si)
            nr = tr + st_s[pl.ds(r0, SUBLANES), rc]
            ni = ti + st_s[pl.ds(r0, SUBLANES), ic]
            st_s[pl.ds(r0, SUBLANES), rc] = nr
            st_s[pl.ds(r0, SUBLANES), ic] = ni
            return nr, ni

        fr, fi = lax.fori_loop(0, KSUB, scan_body, (s0r, s0i))

        pk_re = jnp.broadcast_to(apow_ref[KSUB - 1:KSUB, rc], (SUBLANES, S5_BLK_STATES))
        pk_im = jnp.broadcast_to(apow_ref[KSUB - 1:KSUB, ic], (SUBLANES, S5_BLK_STATES))
        xr = jnp.where(sub512 == 0, 0.0, _sub_roll(fr, 1))
        xi = jnp.where(sub512 == 0, 0.0, _sub_roll(fi, 1))
        qr, qi = pk_re, pk_im
        for dlt in (1, 2, 4):
            shr = jnp.where(sub512 >= dlt, _sub_roll(xr, dlt), 0.0)
            shi = jnp.where(sub512 >= dlt, _sub_roll(xi, dlt), 0.0)
            tr, ti = _cmul(qr, qi, shr, shi)
            xr, xi = xr + tr, xi + ti
            qr, qi = _cmul(qr, qi, qr, qi)
        tr, ti = _cmul(pk_re, pk_im, xr, xi)
        scar_s[:, rc] = fr + tr
        scar_s[:, ic] = fi + ti

        def fix_body(k, carry, rc=rc, ic=ic, xr=xr, xi=xi):
            r0 = pl.multiple_of(k * SUBLANES, SUBLANES)
            pr = jnp.broadcast_to(apow_ref[pl.ds(k, 1), rc], (SUBLANES, S5_BLK_STATES))
            pi = jnp.broadcast_to(apow_ref[pl.ds(k, 1), ic], (SUBLANES, S5_BLK_STATES))
            tr, ti = _cmul(pr, pi, xr, xi)
            st_s[pl.ds(r0, SUBLANES), rc] = st_s[pl.ds(r0, SUBLANES), rc] + tr
            st_s[pl.ds(r0, SUBLANES), ic] = st_s[pl.ds(r0, SUBLANES), ic] + ti
            return carry

        lax.fori_loop(0, KSUB, fix_body, 0)

    for m in range(S5_BLK):
        cs = slice(LANES * m, LANES * (m + 1))
        y = _dot(st_s[:, 1024 * m:1024 * (m + 1)].astype(BF16), cw_ref[m])
        tmp_s[:, cs] = y + d_ref[:, cs] * proj_s[:, cs]
    gl = jax.nn.gelu(tmp_s[...], approximate=True)
    gate = jax.nn.sigmoid(_dot(gl.astype(BF16), wglu_ref[...]) + bglu_ref[...])
    mixed_s[:, 0:MIX_W] = (gl * gate).astype(BF16)

    hc = proj_s[:, MIX_W:2 * MIX_W] * jax.nn.sigmoid(proj_s[:, 2 * MIX_W:3 * MIX_W])
    _fill_ext(cvext_s, cvtail_s, hc, CONV_WIDTH - 1)
    conv_rows = 32

    def conv_body(i, carry):
        r0 = pl.multiple_of(i * conv_rows, conv_rows)
        acc = jnp.broadcast_to(cvb_ref[...], (conv_rows, MIX_W))
        for kk in range(CONV_WIDTH):
            acc = acc + cvw_ref[kk:kk + 1, :] * cvext_s[pl.ds(r0 + SUBLANES * kk, conv_rows), :]
        mu = jnp.mean(acc, axis=-1, keepdims=True)
        cen = acc - mu
        var = jnp.mean(cen * cen, axis=-1, keepdims=True)
        hn = cen * lax.rsqrt(var + EPS) * lng_ref[...] + lnb_ref[...]
        tmp_s[pl.ds(r0, conv_rows), :] = hn * jax.nn.sigmoid(hn)
        return carry

    lax.fori_loop(0, T // conv_rows, conv_body, 0)
    mixed_s[:, MIX_W:2 * MIX_W] = (_dot(tmp_s[...].astype(BF16), wpw_ref[...]) + bpw_ref[...]).astype(BF16)

    _fill_ext(lext_s, ltail_s, proj_s[:, 3 * MIX_W:4 * MIX_W], LRU_CONV_WIDTH - 1)
    xc = jnp.broadcast_to(lb_ref[...], (T, MIX_W))
    for kk in range(LRU_CONV_WIDTH):
        xc = xc + lw_ref[kk:kk + 1, :] * lext_s[SUBLANES * kk:SUBLANES * kk + T, :]
    for hh in range(2):
        cs = slice(256 * hh, 256 * (hh + 1))
        xch = xc[:, cs]
        xcb = xch.astype(BF16)
        r = jax.nn.sigmoid(_dot(xcb, wr_ref[hh]) + br_ref[:, cs])
        ig = jax.nn.sigmoid(_dot(xcb, wi_ref[hh]) + bi_ref[:, cs])
        log_a = -LRU_C * r * sp_ref[:, cs]
        la_s[:, cs] = jnp.exp(log_a)
        th = jnp.tanh(log_a)
        lb_s[:, cs] = jnp.sqrt(-2.0 * th / (1.0 - th)) * (ig * xch)

    h0 = jnp.where(sub512 == 0, _sub_roll(lcar_s[...], 1), 0.0)

    def lru_body(k, carry):
        p, hst = carry
        r0 = pl.multiple_of(k * SUBLANES, SUBLANES)
        a = la_s[pl.ds(r0, SUBLANES), :]
        p = a * p
        hst = a * hst + lb_s[pl.ds(r0, SUBLANES), :]
        la_s[pl.ds(r0, SUBLANES), :] = p
        lb_s[pl.ds(r0, SUBLANES), :] = hst
        return p, hst

    ptot, hfin = lax.fori_loop(0, KSUB, lru_body, (jnp.ones((SUBLANES, MIX_W), F32), h0))
    xin = jnp.where(sub512 == 0, 0.0, _sub_roll(hfin, 1))
    mul = _sub_roll(ptot, 1)
    for dlt in (1, 2, 4):
        xs = jnp.where(sub512 >= dlt, _sub_roll(xin, dlt), 0.0)
        ms = jnp.where(sub512 >= dlt, _sub_roll(mul, dlt), 1.0)
        xin = xin + mul * xs
        mul = mul * ms
    lcar_s[...] = hfin + ptot * xin

    def lru_fix(k, carry):
        r0 = pl.multiple_of(k * SUBLANES, SUBLANES)
        lb_s[pl.ds(r0, SUBLANES), :] = lb_s[pl.ds(r0, SUBLANES), :] + la_s[pl.ds(r0, SUBLANES), :] * xin
        return carry

    lax.fori_loop(0, KSUB, lru_fix, 0)
    mixed_s[:, 2 * MIX_W:3 * MIX_W] = (
        lb_s[...] * jax.nn.gelu(proj_s[:, 4 * MIX_W:5 * MIX_W], approximate=True)).astype(BF16)

    pool_halo = max(POOL_WINDOWS) - 1
    _fill_ext(pext_s, ptail_s, proj_s[:, 5 * MIX_W:6 * MIX_W], pool_halo)
    row = lax.broadcasted_iota(jnp.int32, (T, LANES), 0)
    tok = c * T + (row & (SUBLANES - 1)) * KSUB + (row >> 3)
    base = SUBLANES * pool_halo
    for gi, win in enumerate(POOL_WINDOWS):
        cs = slice(LANES * gi, LANES * (gi + 1))
        cur = pext_s[base:base + T, cs]
        ssum = cur
        for dd in range(1, win):
            ssum = ssum + pext_s[base - SUBLANES * dd:base - SUBLANES * dd + T, cs]
        cnt = jnp.minimum(tok + 1, win).astype(F32)
        dg = ssum / cnt - cur
        y = _dot(dg.astype(BF16), pw_ref[gi]) * ps_ref[:, cs]
        mixed_s[:, 3 * MIX_W + LANES * gi:3 * MIX_W + LANES * (gi + 1)] = y.astype(BF16)

    o_ref[...] = x_ref[...] + _dot(mixed_s[...], wout_ref[...])


def _ffn_kernel(x_ref, g_ref, wg_ref, wv_ref, cw_ref, cb_ref, wd_ref, gf_ref, o_ref,
                h_s, gate_s, gc_s, tail_s, *, final_norm, tiles_per_seq):
    i = pl.program_id(0)
    j = pl.program_id(1)
    n_sub = FFN_TM // T
    pad = 2 * SUBLANES

    @pl.when(j == 0)
    def _():
        h_s[...] = _rms(x_ref[...], g_ref[...]).astype(BF16)

    @pl.when((i == 0) & (j == 0))
    def _():
        tail_s[...] = jnp.zeros_like(tail_s)
        gate_s[0:pad, :] = jnp.zeros((pad, FFN_TF), F32)

    hb = h_s[...]
    gate_s[pad:pad + FFN_TM, :] = _dot(hb, wg_ref[...])
    val = _dot(hb, wv_ref[...])

    w0 = cw_ref[0:1, :]
    w1 = cw_ref[1:2, :]
    w2 = cw_ref[2:3, :]
    cb = cb_ref[...]
    gc_s[...] = (cb + w0 * gate_s[0:FFN_TM, :] + w1 * gate_s[SUBLANES:SUBLANES + FFN_TM, :]
                 + w2 * gate_s[pad:pad + FFN_TM, :])
    sub = lax.broadcasted_iota(jnp.int32, (2, SUBLANES, FFN_TF), 1)
    for s in range(n_sub):
        lo = pad + s * T
        tail = gate_s[lo + T - pad:lo + T, :].reshape(2, SUBLANES, FFN_TF)
        if s == 0:
            prev = tail_s[j]
        else:
            prev = gate_s[lo - pad:lo, :]
        first = ((i * n_sub + s) % tiles_per_seq) == 0
        prev = jnp.where(first, 0.0, prev).reshape(2, SUBLANES, FFN_TF)
        halo = pltpu.roll(jnp.where(sub == SUBLANES - 1, prev, tail), 1, 1)
        g0 = gate_s[lo:lo + SUBLANES, :]
        g1 = gate_s[lo + SUBLANES:lo + pad, :]
        gc_s[s * T:s * T + SUBLANES, :] = cb + w0 * halo[0] + w1 * halo[1] + w2 * g0
        gc_s[s * T + SUBLANES:s * T + pad, :] = cb + w0 * halo[1] + w1 * g0 + w2 * g1
    tail_s[j] = gate_s[FFN_TM:FFN_TM + pad, :]

    act = (jax.nn.gelu(gc_s[...], approximate=True) * val).astype(BF16)
    ncol = 512
    for n in range(D_MODEL // ncol):
        cs = slice(ncol * n, ncol * (n + 1))
        contrib = _dot(act, wd_ref[:, cs])

        @pl.when(j == 0)
        def _(cs=cs, contrib=contrib):
            o_ref[:, cs] = x_ref[:, cs] + contrib

        @pl.when(j > 0)
        def _(cs=cs, contrib=contrib):
            o_ref[:, cs] = o_ref[:, cs] + contrib

    if final_norm:
        @pl.when(j == N_FCHUNK - 1)
        def _():
            o_ref[...] = _rms(o_ref[...], gf_ref[...])


def _const_spec(shape):
    nd = len(shape)
    return pl.BlockSpec(shape, lambda *_: (0,) * nd, pipeline_mode=pl.Buffered(1))


def _mix_layer(xp, p, n_batch, tiles_per_seq):
    n_rows = xp.shape[0]
    params = [p["g"], p["w_in"], p["bw"], p["apow"], p["cw"], p["d"], p["w_glu"], p["b_glu"],
              p["cv_w"], p["cv_b"], p["ln_g"], p["ln_b"], p["w_pw"], p["b_pw"],
              p["l_w"], p["l_b"], p["w_r"], p["b_r"], p["w_i"], p["b_i"], p["sp"],
              p["pool_w"], p["pool_s"], p["w_out"]]
    row_spec = pl.BlockSpec((T, D_MODEL), lambda b, c: (b * tiles_per_seq + c, 0))
    cv_halo = SUBLANES * (CONV_WIDTH - 1)
    l_halo = SUBLANES * (LRU_CONV_WIDTH - 1)
    p_halo = SUBLANES * (max(POOL_WINDOWS) - 1)
    scratch = [
        pltpu.VMEM((T, IN_COLS), F32),
        pltpu.VMEM((T, 2 * S5_BLK * S5_BLK_STATES), F32),
        pltpu.VMEM((SUBLANES, 2 * S5_BLK * S5_BLK_STATES), F32),
        pltpu.VMEM((cv_halo + T, MIX_W), F32),
        pltpu.VMEM((cv_halo, MIX_W), F32),
        pltpu.VMEM((T, MIX_W), F32),
        pltpu.VMEM((l_halo + T, MIX_W), F32),
        pltpu.VMEM((l_halo, MIX_W), F32),
        pltpu.VMEM((T, MIX_W), F32),
        pltpu.VMEM((T, MIX_W), F32),
        pltpu.VMEM((SUBLANES, MIX_W), F32),
        pltpu.VMEM((p_halo + T, MIX_W), F32),
        pltpu.VMEM((p_halo, MIX_W), F32),
        pltpu.VMEM((T, D_MODEL), BF16),
    ]
    return pl.pallas_call(
        _mix_kernel,
        grid=(n_batch, tiles_per_seq),
        in_specs=[row_spec] + [_const_spec(a.shape) for a in params],
        out_specs=row_spec,
        out_shape=jax.ShapeDtypeStruct((n_rows, D_MODEL), F32),
        scratch_shapes=scratch,
        compiler_params=pltpu.CompilerParams(
            dimension_semantics=("arbitrary", "arbitrary"), vmem_limit_bytes=VMEM_LIMIT),
        name="mix_layer",
    )(xp, *params)


def _ffn_layer(xp, p, gf, tiles_per_seq, final_norm):
    n_rows = xp.shape[0]
    row_spec_in = pl.BlockSpec((FFN_TM, D_MODEL), lambda i, j: (i, 0), pipeline_mode=pl.Buffered(1))
    row_spec_out = pl.BlockSpec((FFN_TM, D_MODEL), lambda i, j: (i, 0))
    in_specs = [
        row_spec_in,
        _const_spec((1, D_MODEL)),
        pl.BlockSpec((D_MODEL, FFN_TF), lambda i, j: (0, j)),
        pl.BlockSpec((D_MODEL, FFN_TF), lambda i, j: (0, j)),
        pl.BlockSpec((SUBLANES, FFN_TF), lambda i, j: (0, j)),
        pl.BlockSpec((1, FFN_TF), lambda i, j: (0, j)),
        pl.BlockSpec((FFN_TF, D_MODEL), lambda i, j: (j, 0)),
        _const_spec((1, D_MODEL)),
    ]
    pad = 2 * SUBLANES
    scratch = [
        pltpu.VMEM((FFN_TM, D_MODEL), BF16),
        pltpu.VMEM((pad + FFN_TM, FFN_TF), F32),
        pltpu.VMEM((FFN_TM, FFN_TF), F32),
        pltpu.VMEM((N_FCHUNK, pad, FFN_TF), F32),
    ]
    kern = functools.partial(_ffn_kernel, final_norm=final_norm, tiles_per_seq=tiles_per_seq)
    return pl.pallas_call(
        kern,
        grid=(n_rows // FFN_TM, N_FCHUNK),
        in_specs=in_specs,
        out_specs=row_spec_out,
        out_shape=jax.ShapeDtypeStruct((n_rows, D_MODEL), F32),
        scratch_shapes=scratch,
        compiler_params=pltpu.CompilerParams(
            dimension_semantics=("arbitrary", "arbitrary"), vmem_limit_bytes=VMEM_LIMIT),
        name="ffn_final" if final_norm else "ffn_layer",
    )(xp, p["g_ffn"], p["w_gate"], p["w_val"], p["f_cw"], p["f_cb"], p["w_down"], gf)


def _state_cols(z):
    zr = jnp.real(z).astype(F32).reshape(S5_BLK, S5_BLK_STATES)
    zi = jnp.imag(z).astype(F32).reshape(S5_BLK, S5_BLK_STATES)
    return jnp.stack([zr, zi], axis=1).reshape(-1)


def _block_diag(w):
    n, a, b = w.shape
    eye = jnp.eye(n, dtype=w.dtype)
    return jnp.einsum("nab,nm->namb", w, eye).reshape(n * a, n * b)


def _prep_layer(l, a):
    row = lambda v: v[l].reshape(1, -1).astype(F32)
    p = {}
    p["g"] = row(a["norm_mix_g"])
    p["w_in"] = a["w_in"][l].astype(BF16)
    lam = lax.complex(a["s5_lam_re"][l].astype(F32), a["s5_lam_im"][l].astype(F32))
    step = jnp.exp(a["s5_log_step"][l].astype(F32))[:, None]
    lam_bar = jnp.exp(lam * step)
    bmat = lax.complex(a["s5_b_re"][l].astype(F32), a["s5_b_im"][l].astype(F32))
    b_bar = ((lam_bar - 1.0) / lam)[..., None] * bmat
    bt = jnp.transpose(b_bar, (0, 2, 1))
    bre = jnp.real(bt).reshape(S5_BLK, 8, S5_GROUP_CH, S5_STATE)
    bim = jnp.imag(bt).reshape(S5_BLK, 8, S5_GROUP_CH, S5_STATE)
    bw = jnp.concatenate([jax.vmap(_block_diag)(bre), jax.vmap(_block_diag)(bim)], axis=-1)
    p["bw"] = bw.astype(BF16)
    ks = jnp.arange(1, KSUB + 1, dtype=F32)[:, None, None]
    apow = jnp.exp((lam * step)[None] * ks)
    p["apow"] = jax.vmap(_state_cols)(apow)
    cre = jnp.transpose(a["s5_c_re"][l].astype(F32), (0, 2, 1)).reshape(S5_BLK, 8, S5_STATE, S5_GROUP_CH)
    cim = jnp.transpose(a["s5_c_im"][l].astype(F32), (0, 2, 1)).reshape(S5_BLK, 8, S5_STATE, S5_GROUP_CH)
    cw = jnp.concatenate([jax.vmap(_block_diag)(cre), -jax.vmap(_block_diag)(cim)], axis=1)
    p["cw"] = cw.astype(BF16)
    p["d"] = row(a["s5_d"])
    p["w_glu"] = a["s5_w_glu"][l].astype(BF16)
    p["b_glu"] = row(a["s5_b_glu"])
    p["cv_w"] = a["cv_w_dw"][l].astype(F32)
    p["cv_b"] = row(a["cv_b_dw"])
    p["ln_g"] = row(a["cv_ln_g"])
    p["ln_b"] = row(a["cv_ln_b"])
    p["w_pw"] = a["cv_w_pw"][l].astype(BF16)
    p["b_pw"] = row(a["cv_b_pw"])
    p["l_w"] = a["lru_w_conv"][l].astype(F32)
    p["l_b"] = row(a["lru_b_conv"])
    p["w_r"] = jax.vmap(_block_diag)(a["lru_w_r"][l].reshape(2, 4, LRU_HEAD_DIM, LRU_HEAD_DIM)).astype(BF16)
    p["b_r"] = row(a["lru_b_r"])
    p["w_i"] = jax.vmap(_block_diag)(a["lru_w_i"][l].reshape(2, 4, LRU_HEAD_DIM, LRU_HEAD_DIM)).astype(BF16)
    p["b_i"] = row(a["lru_b_i"])
    p["sp"] = jax.nn.softplus(-row(a["lru_lam"]))
    p["pool_w"] = a["pool_w"][l].astype(BF16)
    p["pool_s"] = row(a["pool_scale"])
    p["w_out"] = a["w_out"][l].astype(BF16)
    p["g_ffn"] = row(a["norm_ffn_g"])
    fpad = FFN_PAD - FFN_DIM
    w_up = a["ffn_w_up"][l]
    p["w_gate"] = jnp.pad(w_up[:, :FFN_DIM], ((0, 0), (0, fpad))).astype(BF16)
    p["w_val"] = jnp.pad(w_up[:, FFN_DIM:], ((0, 0), (0, fpad))).astype(BF16)
    p["f_cw"] = jnp.pad(a["ffn_w_dw"][l].astype(F32), ((0, SUBLANES - FFN_CONV_WIDTH), (0, fpad)))
    p["f_cb"] = jnp.pad(row(a["ffn_b_dw"]), ((0, 0), (0, fpad)))
    p["w_down"] = jnp.pad(a["ffn_w_down"][l], ((0, fpad), (0, 0))).astype(BF16)
    return p


def kernel(x, norm_mix_g, w_in, s5_lam_re, s5_lam_im, s5_log_step, s5_b_re, s5_b_im, s5_c_re, s5_c_im, s5_d, s5_w_glu, s5_b_glu, cv_w_dw, cv_b_dw, cv_ln_g, cv_ln_b, cv_w_pw, cv_b_pw, lru_w_conv, lru_b_conv, lru_w_r, lru_b_r, lru_w_i, lru_b_i, lru_lam, pool_w, pool_scale, w_out, norm_ffn_g, ffn_w_up, ffn_w_dw, ffn_b_dw, ffn_w_down, norm_final_g):
    a = dict(norm_mix_g=norm_mix_g, w_in=w_in, s5_lam_re=s5_lam_re, s5_lam_im=s5_lam_im,
             s5_log_step=s5_log_step, s5_b_re=s5_b_re, s5_b_im=s5_b_im, s5_c_re=s5_c_re,
             s5_c_im=s5_c_im, s5_d=s5_d, s5_w_glu=s5_w_glu, s5_b_glu=s5_b_glu, cv_w_dw=cv_w_dw,
             cv_b_dw=cv_b_dw, cv_ln_g=cv_ln_g, cv_ln_b=cv_ln_b, cv_w_pw=cv_w_pw, cv_b_pw=cv_b_pw,
             lru_w_conv=lru_w_conv, lru_b_conv=lru_b_conv, lru_w_r=lru_w_r, lru_b_r=lru_b_r,
             lru_w_i=lru_w_i, lru_b_i=lru_b_i, lru_lam=lru_lam, pool_w=pool_w,
             pool_scale=pool_scale, w_out=w_out, norm_ffn_g=norm_ffn_g, ffn_w_up=ffn_w_up,
             ffn_w_dw=ffn_w_dw, ffn_b_dw=ffn_b_dw, ffn_w_down=ffn_w_down)
    n_batch, seq, d = x.shape
    depth = w_in.shape[0]
    assert d == D_MODEL and seq % T == 0 and (n_batch * seq) % FFN_TM == 0 and seq % FFN_TM == 0
    tiles_per_seq = seq // T
    xp = x.astype(F32).reshape(n_batch, tiles_per_seq, SUBLANES, KSUB, d)
    xp = jnp.transpose(xp, (0, 1, 3, 2, 4)).reshape(n_batch * seq, d)
    gf = norm_final_g.reshape(1, -1).astype(F32)
    for l in range(depth):
        p = _prep_layer(l, a)
        xp = _mix_layer(xp, p, n_batch, tiles_per_seq)
        xp = _ffn_layer(xp, p, gf, tiles_per_seq, final_norm=(l == depth - 1))
    out = xp.reshape(n_batch, tiles_per_seq, KSUB, SUBLANES, d)
    out = jnp.transpose(out, (0, 1, 3, 2, 4)).reshape(n_batch, seq, d)
    return out.astype(x.dtype)
```

```python
import functools

import jax
import jax.numpy as jnp
from jax import lax
from jax.experimental import pallas as pl
from jax.experimental.pallas import tpu as pltpu

D_MODEL = 2048
MIX_W = 512
IN_COLS = 6 * MIX_W
S5_GROUP_CH = 16
S5_STATE = 64
CONV_WIDTH = 31
LRU_HEADS = 8
LRU_HEAD_DIM = MIX_W // LRU_HEADS
LRU_CONV_WIDTH = 4
LRU_C = 8.0
POOL_WINDOWS = (2, 4, 8, 16)
FFN_DIM = 5504
FFN_CONV_WIDTH = 3
EPS = 1e-6

SUBLANES = 8
LANES = 128

T = 256
KSUB = T // SUBLANES
FFN_TM = 1024
FFN_TF = 512
FFN_PAD = 5632
N_FCHUNK = FFN_PAD // FFN_TF
S5_BLK = 4
S5_BLK_STATES = 512
SCAN_UNROLL = 4
VMEM_LIMIT = 56 * 1024 * 1024

F32 = jnp.float32
BF16 = jnp.bfloat16


def _dot(a, b):
    return jnp.dot(a, b, preferred_element_type=F32)


def _rms(x, g):
    ms = jnp.mean(x * x, axis=-1, keepdims=True)
    return x * lax.rsqrt(ms + EPS) * g


def _cmul(ar, ai, br, bi):
    return ar * br - ai * bi, ar * bi + ai * br


def _sub_roll(x, d):
    return pltpu.roll(x, d, 0)


def _fill_ext(ext_ref, tail_ref, a, dv):
    w = a.shape[1]
    n = SUBLANES * dv
    ext_ref[n:n + T, :] = a
    tail = a[T - n:, :].reshape(dv, SUBLANES, w)
    prev = tail_ref[...].reshape(dv, SUBLANES, w)
    sub = lax.broadcasted_iota(jnp.int32, (dv, SUBLANES, w), 1)
    halo = pltpu.roll(jnp.where(sub == SUBLANES - 1, prev, tail), 1, 1)
    ext_ref[0:n, :] = halo.reshape(n, w)
    tail_ref[...] = a[T - n:, :]


def _mix_kernel(x_ref, g_ref, win_ref, bw_ref, apow_ref, cw_ref, d_ref, wglu_ref, bglu_ref,
                cvw_ref, cvb_ref, lng_ref, lnb_ref, wpw_ref, bpw_ref,
                lw_ref, lb_ref, wr_ref, br_ref, wi_ref, bi_ref, sp_ref,
                pw_ref, ps_ref, wout_ref, o_ref,
                proj_s, st_s, scar_s, cvext_s, cvtail_s, tmp_s,
                lext_s, ltail_s, la_s, lb_s, lcar_s, pext_s, ptail_s, mixed_s):
    c = pl.program_id(1)

    @pl.when(c == 0)
    def _():
        scar_s[...] = jnp.zeros_like(scar_s)
        cvtail_s[...] = jnp.zeros_like(cvtail_s)
        ltail_s[...] = jnp.zeros_like(ltail_s)
        lcar_s[...] = jnp.zeros_like(lcar_s)
        ptail_s[...] = jnp.zeros_like(ptail_s)

    h = _rms(x_ref[...], g_ref[...]).astype(BF16)
    proj_s[...] = _dot(h, win_ref[...])

    sub512 = lax.broadcasted_iota(jnp.int32, (SUBLANES, MIX_W), 0)

    for m in range(S5_BLK):
        u = proj_s[:, LANES * m:LANES * (m + 1)]
        st_s[:, 1024 * m:1024 * (m + 1)] = _dot(u.astype(BF16), bw_ref[m])

    for m in range(S5_BLK):
        rc = slice(1024 * m, 1024 * m + S5_BLK_STATES)
        ic = slice(1024 * m + S5_BLK_STATES, 1024 * (m + 1))
        a_re = jnp.broadcast_to(apow_ref[0:1, rc], (SUBLANES, S5_BLK_STATES))
        a_im = jnp.broadcast_to(apow_ref[0:1, ic], (SUBLANES, S5_BLK_STATES))
        s0r = jnp.where(sub512 == 0, _sub_roll(scar_s[:, rc], 1), 0.0)
        s0i = jnp.where(sub512 == 0, _sub_roll(scar_s[:, ic], 1), 0.0)

        def scan_body(k, carry, rc=rc, ic=ic, a_re=a_re, a_im=a_im):
            sr, si = carry
            r0 = pl.multiple_of(k * SUBLANES, SUBLANES)
            tr, ti = _cmul(a_re, a_im, sr, si)
            nr = tr + st_s[pl.ds(r0, SUBLANES), rc]
            ni = ti + st_s[pl.ds(r0, SUBLANES), ic]
            st_s[pl.ds(r0, SUBLANES), rc] = nr
            st_s[pl.ds(r0, SUBLANES), ic] = ni
            return nr, ni

        fr, fi = lax.fori_loop(0, KSUB, scan_body, (s0r, s0i), unroll=SCAN_UNROLL)

        pk_re = jnp.broadcast_to(apow_ref[KSUB - 1:KSUB, rc], (SUBLANES, S5_BLK_STATES))
        pk_im = jnp.broadcast_to(apow_ref[KSUB - 1:KSUB, ic], (SUBLANES, S5_BLK_STATES))
        xr = jnp.where(sub512 == 0, 0.0, _sub_roll(fr, 1))
        xi = jnp.where(sub512 == 0, 0.0, _sub_roll(fi, 1))
        qr, qi = pk_re, pk_im
        for dlt in (1, 2, 4):
            shr = jnp.where(sub512 >= dlt, _sub_roll(xr, dlt), 0.0)
            shi = jnp.where(sub512 >= dlt, _sub_roll(xi, dlt), 0.0)
            tr, ti = _cmul(qr, qi, shr, shi)
            xr, xi = xr + tr, xi + ti
            qr, qi = _cmul(qr, qi, qr, qi)
        tr, ti = _cmul(pk_re, pk_im, xr, xi)
        scar_s[:, rc] = fr + tr
        scar_s[:, ic] = fi + ti

        def fix_body(k, carry, rc=rc, ic=ic, xr=xr, xi=xi):
            r0 = pl.multiple_of(k * SUBLANES, SUBLANES)
            pr = jnp.broadcast_to(apow_ref[pl.ds(k, 1), rc], (SUBLANES, S5_BLK_STATES))
            pi = jnp.broadcast_to(apow_ref[pl.ds(k, 1), ic], (SUBLANES, S5_BLK_STATES))
            tr, ti = _cmul(pr, pi, xr, xi)
            st_s[pl.ds(r0, SUBLANES), rc] = st_s[pl.ds(r0, SUBLANES), rc] + tr
            st_s[pl.ds(r0, SUBLANES), ic] = st_s[pl.ds(r0, SUBLANES), ic] + ti
            return carry

        lax.fori_loop(0, KSUB, fix_body, 0, unroll=SCAN_UNROLL)

    for m in range(S5_BLK):
        cs = slice(LANES * m, LANES * (m + 1))
        y = _dot(st_s[:, 1024 * m:1024 * (m + 1)].astype(BF16), cw_ref[m])
        tmp_s[:, cs] = y + d_ref[:, cs] * proj_s[:, cs]
    gl = jax.nn.gelu(tmp_s[...], approximate=True)
    gate = jax.nn.sigmoid(_dot(gl.astype(BF16), wglu_ref[...]) + bglu_ref[...])
    mixed_s[:, 0:MIX_W] = (gl * gate).astype(BF16)

    hc = proj_s[:, MIX_W:2 * MIX_W] * jax.nn.sigmoid(proj_s[:, 2 * MIX_W:3 * MIX_W])
    _fill_ext(cvext_s, cvtail_s, hc, CONV_WIDTH - 1)
    conv_rows = 32

    def conv_body(i, carry):
        r0 = pl.multiple_of(i * conv_rows, conv_rows)
        acc = jnp.broadcast_to(cvb_ref[...], (conv_rows, MIX_W))
        for kk in range(CONV_WIDTH):
            acc = acc + cvw_ref[kk:kk + 1, :] * cvext_s[pl.ds(r0 + SUBLANES * kk, conv_rows), :]
        tmp_s[pl.ds(r0, conv_rows), :] = acc
        return carry

    lax.fori_loop(0, T // conv_rows, conv_body, 0)
    acc = tmp_s[...]
    mu = jnp.mean(acc, axis=-1, keepdims=True)
    cen = acc - mu
    var = jnp.mean(cen * cen, axis=-1, keepdims=True)
    hn = cen * lax.rsqrt(var + EPS) * lng_ref[...] + lnb_ref[...]
    hs = (hn * jax.nn.sigmoid(hn)).astype(BF16)
    mixed_s[:, MIX_W:2 * MIX_W] = (_dot(hs, wpw_ref[...]) + bpw_ref[...]).astype(BF16)

    _fill_ext(lext_s, ltail_s, proj_s[:, 3 * MIX_W:4 * MIX_W], LRU_CONV_WIDTH - 1)
    xc = jnp.broadcast_to(lb_ref[...], (T, MIX_W))
    for kk in range(LRU_CONV_WIDTH):
        xc = xc + lw_ref[kk:kk + 1, :] * lext_s[SUBLANES * kk:SUBLANES * kk + T, :]
    for hh in range(2):
        cs = slice(256 * hh, 256 * (hh + 1))
        xch = xc[:, cs]
        xcb = xch.astype(BF16)
        r = jax.nn.sigmoid(_dot(xcb, wr_ref[hh]) + br_ref[:, cs])
        ig = jax.nn.sigmoid(_dot(xcb, wi_ref[hh]) + bi_ref[:, cs])
        log_a = -LRU_C * r * sp_ref[:, cs]
        la_s[:, cs] = jnp.exp(log_a)
        th = jnp.tanh(log_a)
        lb_s[:, cs] = jnp.sqrt(-2.0 * th / (1.0 - th)) * (ig * xch)

    h0 = jnp.where(sub512 == 0, _sub_roll(lcar_s[...], 1), 0.0)

    def lru_body(k, carry):
        p, hst = carry
        r0 = pl.multiple_of(k * SUBLANES, SUBLANES)
        a = la_s[pl.ds(r0, SUBLANES), :]
        p = a * p
        hst = a * hst + lb_s[pl.ds(r0, SUBLANES), :]
        la_s[pl.ds(r0, SUBLANES), :] = p
        lb_s[pl.ds(r0, SUBLANES), :] = hst
        return p, hst

    ptot, hfin = lax.fori_loop(0, KSUB, lru_body, (jnp.ones((SUBLANES, MIX_W), F32), h0),
                               unroll=SCAN_UNROLL)
    xin = jnp.where(sub512 == 0, 0.0, _sub_roll(hfin, 1))
    mul = _sub_roll(ptot, 1)
    for dlt in (1, 2, 4):
        xs = jnp.where(sub512 >= dlt, _sub_roll(xin, dlt), 0.0)
        ms = jnp.where(sub512 >= dlt, _sub_roll(mul, dlt), 1.0)
        xin = xin + mul * xs
        mul = mul * ms
    lcar_s[...] = hfin + ptot * xin

    def lru_fix(k, carry):
        r0 = pl.multiple_of(k * SUBLANES, SUBLANES)
        lb_s[pl.ds(r0, SUBLANES), :] = lb_s[pl.ds(r0, SUBLANES), :] + la_s[pl.ds(r0, SUBLANES), :] * xin
        return carry

    lax.fori_loop(0, KSUB, lru_fix, 0, unroll=SCAN_UNROLL)
    mixed_s[:, 2 * MIX_W:3 * MIX_W] = (
        lb_s[...] * jax.nn.gelu(proj_s[:, 4 * MIX_W:5 * MIX_W], approximate=True)).astype(BF16)

    pool_halo = max(POOL_WINDOWS) - 1
    _fill_ext(pext_s, ptail_s, proj_s[:, 5 * MIX_W:6 * MIX_W], pool_halo)
    row = lax.broadcasted_iota(jnp.int32, (T, LANES), 0)
    tok = c * T + (row & (SUBLANES - 1)) * KSUB + (row >> 3)
    base = SUBLANES * pool_halo
    for gi, win in enumerate(POOL_WINDOWS):
        cs = slice(LANES * gi, LANES * (gi + 1))
        cur = pext_s[base:base + T, cs]
        ssum = cur
        for dd in range(1, win):
            ssum = ssum + pext_s[base - SUBLANES * dd:base - SUBLANES * dd + T, cs]
        cnt = jnp.minimum(tok + 1, win).astype(F32)
        dg = ssum / cnt - cur
        y = _dot(dg.astype(BF16), pw_ref[gi]) * ps_ref[:, cs]
        mixed_s[:, 3 * MIX_W + LANES * gi:3 * MIX_W + LANES * (gi + 1)] = y.astype(BF16)

    o_ref[...] = x_ref[...] + _dot(mixed_s[...], wout_ref[...])


def _ffn_kernel(x_ref, g_ref, wg_ref, wv_ref, cw_ref, cb_ref, wd_ref, gf_ref, o_ref,
                h_s, gate0_s, gate1_s, val0_s, val1_s, act_s, tail_s, *, final_norm, tiles_per_seq):
    i = pl.program_id(0)
    j = pl.program_id(1)
    n_sub = FFN_TM // T
    pad = 2 * SUBLANES
    gates = (gate0_s, gate1_s)
    vals = (val0_s, val1_s)

    def up_proj(par):
        hb = h_s[...]
        gates[par][pad:pad + FFN_TM, :] = _dot(hb, wg_ref[...])
        vals[par][...] = _dot(hb, wv_ref[...])

    def down_proj(par, jj):
        g_s = gates[par]
        v_s = vals[par]
        w0 = cw_ref[0:1, :]
        w1 = cw_ref[1:2, :]
        w2 = cw_ref[2:3, :]
        cb = cb_ref[...]
        gc = (cb + w0 * g_s[0:FFN_TM, :] + w1 * g_s[SUBLANES:SUBLANES + FFN_TM, :]
              + w2 * g_s[pad:pad + FFN_TM, :])
        act_s[...] = (jax.nn.gelu(gc, approximate=True) * v_s[...]).astype(BF16)
        sub = lax.broadcasted_iota(jnp.int32, (2, SUBLANES, FFN_TF), 1)
        for s in range(n_sub):
            lo = pad + s * T
            tail = g_s[lo + T - pad:lo + T, :].reshape(2, SUBLANES, FFN_TF)
            prev = tail_s[jj] if s == 0 else g_s[lo - pad:lo, :]
            first = ((i * n_sub + s) % tiles_per_seq) == 0
            prev = jnp.where(first, 0.0, prev).reshape(2, SUBLANES, FFN_TF)
            halo = pltpu.roll(jnp.where(sub == SUBLANES - 1, prev, tail), 1, 1)
            g0 = g_s[lo:lo + SUBLANES, :]
            g1 = g_s[lo + SUBLANES:lo + pad, :]
            fix = jnp.concatenate([cb + w0 * halo[0] + w1 * halo[1] + w2 * g0,
                                   cb + w0 * halo[1] + w1 * g0 + w2 * g1], axis=0)
            act_s[s * T:s * T + pad, :] = (
                jax.nn.gelu(fix, approximate=True) * v_s[s * T:s * T + pad, :]).astype(BF16)
        tail_s[jj] = g_s[FFN_TM:FFN_TM + pad, :]
        ncol = 512
        for n in range(D_MODEL // ncol):
            cs = slice(ncol * n, ncol * (n + 1))
            o_ref[:, cs] = o_ref[:, cs] + _dot(act_s[...], wd_ref[:, cs])

    @pl.when((i == 0) & (j == 0))
    def _():
        tail_s[...] = jnp.zeros_like(tail_s)
        gate0_s[0:pad, :] = jnp.zeros((pad, FFN_TF), F32)
        gate1_s[0:pad, :] = jnp.zeros((pad, FFN_TF), F32)

    @pl.when(j == 0)
    def _():
        x = x_ref[...]
        h_s[...] = _rms(x, g_ref[...]).astype(BF16)
        o_ref[...] = x
        up_proj(0)

    for par in range(2):
        @pl.when((j > 0) & (j < N_FCHUNK) & (j % 2 == par))
        def _(par=par):
            up_proj(par)
            down_proj(1 - par, j - 1)

    @pl.when(j == N_FCHUNK)
    def _():
        down_proj((N_FCHUNK - 1) % 2, N_FCHUNK - 1)
        if final_norm:
            o_ref[...] = _rms(o_ref[...], gf_ref[...])


def _const_spec(shape):
    nd = len(shape)
    return pl.BlockSpec(shape, lambda *_: (0,) * nd, pipeline_mode=pl.Buffered(1))


def _mix_layer(xp, p, n_batch, tiles_per_seq):
    n_rows = xp.shape[0]
    params = [p["g"], p["w_in"], p["bw"], p["apow"], p["cw"], p["d"], p["w_glu"], p["b_glu"],
              p["cv_w"], p["cv_b"], p["ln_g"], p["ln_b"], p["w_pw"], p["b_pw"],
              p["l_w"], p["l_b"], p["w_r"], p["b_r"], p["w_i"], p["b_i"], p["sp"],
              p["pool_w"], p["pool_s"], p["w_out"]]
    row_spec = pl.BlockSpec((T, D_MODEL), lambda b, c: (b * tiles_per_seq + c, 0))
    cv_halo = SUBLANES * (CONV_WIDTH - 1)
    l_halo = SUBLANES * (LRU_CONV_WIDTH - 1)
    p_halo = SUBLANES * (max(POOL_WINDOWS) - 1)
    scratch = [
        pltpu.VMEM((T, IN_COLS), F32),
        pltpu.VMEM((T, 2 * S5_BLK * S5_BLK_STATES), F32),
        pltpu.VMEM((SUBLANES, 2 * S5_BLK * S5_BLK_STATES), F32),
        pltpu.VMEM((cv_halo + T, MIX_W), F32),
        pltpu.VMEM((cv_halo, MIX_W), F32),
        pltpu.VMEM((T, MIX_W), F32),
        pltpu.VMEM((l_halo + T, MIX_W), F32),
        pltpu.VMEM((l_halo, MIX_W), F32),
        pltpu.VMEM((T, MIX_W), F32),
        pltpu.VMEM((T, MIX_W), F32),
        pltpu.VMEM((SUBLANES, MIX_W), F32),
        pltpu.VMEM((p_halo + T, MIX_W), F32),
        pltpu.VMEM((p_halo, MIX_W), F32),
        pltpu.VMEM((T, D_MODEL), BF16),
    ]
    return pl.pallas_call(
        _mix_kernel,
        grid=(n_batch, tiles_per_seq),
        in_specs=[row_spec] + [_const_spec(a.shape) for a in params],
        out_specs=row_spec,
        out_shape=jax.ShapeDtypeStruct((n_rows, D_MODEL), F32),
        scratch_shapes=scratch,
        compiler_params=pltpu.CompilerParams(
            dimension_semantics=("arbitrary", "arbitrary"), vmem_limit_bytes=VMEM_LIMIT),
        name="mix_layer",
    )(xp, *params)


def _ffn_layer(xp, p, gf, tiles_per_seq, final_norm):
    n_rows = xp.shape[0]
    row_spec_in = pl.BlockSpec((FFN_TM, D_MODEL), lambda i, j: (i, 0), pipeline_mode=pl.Buffered(1))
    row_spec_out = pl.BlockSpec((FFN_TM, D_MODEL), lambda i, j: (i, 0))
    last = N_FCHUNK - 1
    up_idx = lambda j: jnp.minimum(j, last)
    dn_idx = lambda j: jnp.maximum(j - 1, 0)
    in_specs = [
        row_spec_in,
        _const_spec((1, D_MODEL)),
        pl.BlockSpec((D_MODEL, FFN_TF), lambda i, j: (0, up_idx(j))),
        pl.BlockSpec((D_MODEL, FFN_TF), lambda i, j: (0, N_FCHUNK + up_idx(j))),
        pl.BlockSpec((SUBLANES, FFN_TF), lambda i, j: (0, dn_idx(j))),
        pl.BlockSpec((1, FFN_TF), lambda i, j: (0, dn_idx(j))),
        pl.BlockSpec((FFN_TF, D_MODEL), lambda i, j: (dn_idx(j), 0)),
        _const_spec((1, D_MODEL)),
    ]
    pad = 2 * SUBLANES
    scratch = [
        pltpu.VMEM((FFN_TM, D_MODEL), BF16),
        pltpu.VMEM((pad + FFN_TM, FFN_TF), F32),
        pltpu.VMEM((pad + FFN_TM, FFN_TF), F32),
        pltpu.VMEM((FFN_TM, FFN_TF), F32),
        pltpu.VMEM((FFN_TM, FFN_TF), F32),
        pltpu.VMEM((FFN_TM, FFN_TF), BF16),
        pltpu.VMEM((N_FCHUNK, pad, FFN_TF), F32),
    ]
    kern = functools.partial(_ffn_kernel, final_norm=final_norm, tiles_per_seq=tiles_per_seq)
    return pl.pallas_call(
        kern,
        grid=(n_rows // FFN_TM, N_FCHUNK + 1),
        in_specs=in_specs,
        out_specs=row_spec_out,
        out_shape=jax.ShapeDtypeStruct((n_rows, D_MODEL), F32),
        scratch_shapes=scratch,
        compiler_params=pltpu.CompilerParams(
            dimension_semantics=("arbitrary", "arbitrary"), vmem_limit_bytes=VMEM_LIMIT),
        name="ffn_final" if final_norm else "ffn_layer",
    )(xp, p["g_ffn"], p["w_up"], p["w_up"], p["f_cw"], p["f_cb"], p["w_down"], gf)


def _state_cols(z):
    zr = jnp.real(z).astype(F32).reshape(S5_BLK, S5_BLK_STATES)
    zi = jnp.imag(z).astype(F32).reshape(S5_BLK, S5_BLK_STATES)
    return jnp.stack([zr, zi], axis=1).reshape(-1)


def _block_diag(w):
    n, a, b = w.shape
    eye = jnp.eye(n, dtype=w.dtype)
    return jnp.einsum("nab,nm->namb", w, eye).reshape(n * a, n * b)


def _prep_layer(l, a):
    row = lambda v: v[l].reshape(1, -1).astype(F32)
    p = {}
    p["g"] = row(a["norm_mix_g"])
    p["w_in"] = a["w_in"][l].astype(BF16)
    lam = lax.complex(a["s5_lam_re"][l].astype(F32), a["s5_lam_im"][l].astype(F32))
    step = jnp.exp(a["s5_log_step"][l].astype(F32))[:, None]
    lam_bar = jnp.exp(lam * step)
    bmat = lax.complex(a["s5_b_re"][l].astype(F32), a["s5_b_im"][l].astype(F32))
    b_bar = ((lam_bar - 1.0) / lam)[..., None] * bmat
    bt = jnp.transpose(b_bar, (0, 2, 1))
    bre = jnp.real(bt).reshape(S5_BLK, 8, S5_GROUP_CH, S5_STATE)
    bim = jnp.imag(bt).reshape(S5_BLK, 8, S5_GROUP_CH, S5_STATE)
    bw = jnp.concatenate([jax.vmap(_block_diag)(bre), jax.vmap(_block_diag)(bim)], axis=-1)
    p["bw"] = bw.astype(BF16)
    ks = jnp.arange(1, KSUB + 1, dtype=F32)[:, None, None]
    apow = jnp.exp((lam * step)[None] * ks)
    p["apow"] = jax.vmap(_state_cols)(apow)
    cre = jnp.transpose(a["s5_c_re"][l].astype(F32), (0, 2, 1)).reshape(S5_BLK, 8, S5_STATE, S5_GROUP_CH)
    cim = jnp.transpose(a["s5_c_im"][l].astype(F32), (0, 2, 1)).reshape(S5_BLK, 8, S5_STATE, S5_GROUP_CH)
    cw = jnp.concatenate([jax.vmap(_block_diag)(cre), -jax.vmap(_block_diag)(cim)], axis=1)
    p["cw"] = cw.astype(BF16)
    p["d"] = row(a["s5_d"])
    p["w_glu"] = a["s5_w_glu"][l].astype(BF16)
    p["b_glu"] = row(a["s5_b_glu"])
    p["cv_w"] = a["cv_w_dw"][l].astype(F32)
    p["cv_b"] = row(a["cv_b_dw"])
    p["ln_g"] = row(a["cv_ln_g"])
    p["ln_b"] = row(a["cv_ln_b"])
    p["w_pw"] = a["cv_w_pw"][l].astype(BF16)
    p["b_pw"] = row(a["cv_b_pw"])
    p["l_w"] = a["lru_w_conv"][l].astype(F32)
    p["l_b"] = row(a["lru_b_conv"])
    p["w_r"] = jax.vmap(_block_diag)(a["lru_w_r"][l].reshape(2, 4, LRU_HEAD_DIM, LRU_HEAD_DIM)).astype(BF16)
    p["b_r"] = row(a["lru_b_r"])
    p["w_i"] = jax.vmap(_block_diag)(a["lru_w_i"][l].reshape(2, 4, LRU_HEAD_DIM, LRU_HEAD_DIM)).astype(BF16)
    p["b_i"] = row(a["lru_b_i"])
    p["sp"] = jax.nn.softplus(-row(a["lru_lam"]))
    p["pool_w"] = a["pool_w"][l].astype(BF16)
    p["pool_s"] = row(a["pool_scale"])
    p["w_out"] = a["w_out"][l].astype(BF16)
    p["g_ffn"] = row(a["norm_ffn_g"])
    fpad = FFN_PAD - FFN_DIM
    w_up = a["ffn_w_up"][l].astype(BF16)
    zcols = jnp.zeros((D_MODEL, fpad), BF16)
    p["w_up"] = jnp.concatenate([w_up[:, :FFN_DIM], zcols, w_up[:, FFN_DIM:], zcols], axis=1)
    p["f_cw"] = jnp.pad(a["ffn_w_dw"][l].astype(F32), ((0, SUBLANES - FFN_CONV_WIDTH), (0, fpad)))
    p["f_cb"] = jnp.pad(row(a["ffn_b_dw"]), ((0, 0), (0, fpad)))
    p["w_down"] = jnp.pad(a["ffn_w_down"][l].astype(BF16), ((0, fpad), (0, 0)))
    return p


def kernel(x, norm_mix_g, w_in, s5_lam_re, s5_lam_im, s5_log_step, s5_b_re, s5_b_im, s5_c_re, s5_c_im, s5_d, s5_w_glu, s5_b_glu, cv_w_dw, cv_b_dw, cv_ln_g, cv_ln_b, cv_w_pw, cv_b_pw, lru_w_conv, lru_b_conv, lru_w_r, lru_b_r, lru_w_i, lru_b_i, lru_lam, pool_w, pool_scale, w_out, norm_ffn_g, ffn_w_up, ffn_w_dw, ffn_b_dw, ffn_w_down, norm_final_g):
    a = dict(norm_mix_g=norm_mix_g, w_in=w_in, s5_lam_re=s5_lam_re, s5_lam_im=s5_lam_im,
             s5_log_step=s5_log_step, s5_b_re=s5_b_re, s5_b_im=s5_b_im, s5_c_re=s5_c_re,
             s5_c_im=s5_c_im, s5_d=s5_d, s5_w_glu=s5_w_glu, s5_b_glu=s5_b_glu, cv_w_dw=cv_w_dw,
             cv_b_dw=cv_b_dw, cv_ln_g=cv_ln_g, cv_ln_b=cv_ln_b, cv_w_pw=cv_w_pw, cv_b_pw=cv_b_pw,
             lru_w_conv=lru_w_conv, lru_b_conv=lru_b_conv, lru_w_r=lru_w_r, lru_b_r=lru_b_r,
             lru_w_i=lru_w_i, lru_b_i=lru_b_i, lru_lam=lru_lam, pool_w=pool_w,
             pool_scale=pool_scale, w_out=w_out, norm_ffn_g=norm_ffn_g, ffn_w_up=ffn_w_up,
             ffn_w_dw=ffn_w_dw, ffn_b_dw=ffn_b_dw, ffn_w_down=ffn_w_down)
    n_batch, seq, d = x.shape
    depth = w_in.shape[0]
    assert d == D_MODEL and seq % T == 0 and (n_batch * seq) % FFN_TM == 0 and seq % FFN_TM == 0
    tiles_per_seq = seq // T
    xp = x.astype(F32).reshape(n_batch, tiles_per_seq, SUBLANES, KSUB, d)
    xp = jnp.transpose(xp, (0, 1, 3, 2, 4)).reshape(n_batch * seq, d)
    gf = norm_final_g.reshape(1, -1).astype(F32)
    for l in range(depth):
        p = _prep_layer(l, a)
        xp = _mix_layer(xp, p, n_batch, tiles_per_seq)
        xp = _ffn_layer(xp, p, gf, tiles_per_seq, final_norm=(l == depth - 1))
    out = xp.reshape(n_batch, tiles_per_seq, KSUB, SUBLANES, d)
    out = jnp.transpose(out, (0, 1, 3, 2, 4)).reshape(n_batch, seq, d)
    return out.astype(x.dtype)
```

```python
import functools

import jax
import jax.numpy as jnp
from jax import lax
from jax.experimental import pallas as pl
from jax.experimental.pallas import tpu as pltpu

D_MODEL = 2048
MIX_W = 512
IN_COLS = 6 * MIX_W
S5_GROUP_CH = 16
S5_STATE = 64
CONV_WIDTH = 31
LRU_HEADS = 8
LRU_HEAD_DIM = MIX_W // LRU_HEADS
LRU_CONV_WIDTH = 4
LRU_C = 8.0
POOL_WINDOWS = (2, 4, 8, 16)
FFN_DIM = 5504
FFN_CONV_WIDTH = 3
EPS = 1e-6

SUBLANES = 8
LANES = 128

T = 256
KSUB = T // SUBLANES
FFN_TM = 1024
FFN_TF = 512
FFN_PAD = 5632
N_FCHUNK = FFN_PAD // FFN_TF
S5_BLK = 4
S5_BLK_STATES = 512
S5_COLS = 2 * S5_BLK * S5_BLK_STATES
CONV_ROWS = 32
VMEM_LIMIT = 56 * 1024 * 1024

MID_U, MID_X, MID_G = 0, MIX_W, 2 * MIX_W
MIDF_COLS = 3 * MIX_W
MIDB_COLS = 2 * MIX_W

F32 = jnp.float32
BF16 = jnp.bfloat16


def _dot(a, b):
    return jnp.dot(a, b, preferred_element_type=F32)


def _rms(x, g):
    ms = jnp.mean(x * x, axis=-1, keepdims=True)
    return x * lax.rsqrt(ms + EPS) * g


def _cmul(ar, ai, br, bi):
    return ar * br - ai * bi, ar * bi + ai * br


def _sub_roll(x, d):
    return pltpu.roll(x, d, 0)


def _rows(k):
    return slice(SUBLANES * k, SUBLANES * (k + 1))


def _fill_ext(ext_ref, tail_ref, a, dv, first):
    w = a.shape[1]
    n = SUBLANES * dv
    ext_ref[n:n + T, :] = a
    tail = a[T - n:, :].reshape(dv, SUBLANES, w)
    prev = jnp.where(first, 0.0, tail_ref[...]).reshape(dv, SUBLANES, w)
    sub = lax.broadcasted_iota(jnp.int32, (dv, SUBLANES, w), 1)
    halo = pltpu.roll(jnp.where(sub == SUBLANES - 1, prev, tail), 1, 1)
    ext_ref[0:n, :] = halo.reshape(n, w)
    tail_ref[...] = a[T - n:, :]


def _front_kernel(x_ref, g_ref, win_ref, cvw_ref, cvb_ref, lng_ref, lnb_ref, wpw_ref, bpw_ref,
                  lw_ref, lb_ref, pw_ref, ps_ref,
                  midf_ref, midb_ref,
                  proj_s, cvext_s, cvtail_s, hs_s, lext_s, ltail_s, pext_s, ptail_s,
                  *, n_tiles, tiles_per_seq):
    s = pl.program_id(0)
    cv_halo = CONV_WIDTH - 1
    l_halo = LRU_CONV_WIDTH - 1
    p_halo = max(POOL_WINDOWS) - 1
    n_grp = CONV_ROWS // SUBLANES

    def project():
        h = _rms(x_ref[...], g_ref[...]).astype(BF16)
        proj_s[...] = _dot(h, win_ref[...])

    def gather(t):
        first = (t % tiles_per_seq) == 0
        midf_ref[:, MID_U:MID_U + MIX_W] = proj_s[:, 0:MIX_W]
        hc = proj_s[:, MIX_W:2 * MIX_W] * jax.nn.sigmoid(proj_s[:, 2 * MIX_W:3 * MIX_W])
        _fill_ext(cvext_s, cvtail_s, hc, cv_halo, first)
        _fill_ext(lext_s, ltail_s, proj_s[:, 3 * MIX_W:4 * MIX_W], l_halo, first)
        midf_ref[:, MID_G:MID_G + MIX_W] = jax.nn.gelu(proj_s[:, 4 * MIX_W:5 * MIX_W], approximate=True)
        _fill_ext(pext_s, ptail_s, proj_s[:, 5 * MIX_W:6 * MIX_W], p_halo, first)

    def mixers(t):
        cvb = jnp.broadcast_to(cvb_ref[...], (SUBLANES, MIX_W))
        lbb = jnp.broadcast_to(lb_ref[...], (SUBLANES, MIX_W))
        for rb in range(T // CONV_ROWS):
            r0 = rb * CONV_ROWS
            accs = [cvb] * n_grp
            for kk in range(CONV_WIDTH):
                w8 = cvw_ref[_rows(kk), :]
                for gq in range(n_grp):
                    lo = r0 + SUBLANES * (gq + kk)
                    accs[gq] = accs[gq] + w8 * cvext_s[lo:lo + SUBLANES, :]
            acc = jnp.concatenate(accs, axis=0)
            mu = jnp.mean(acc, axis=-1, keepdims=True)
            cen = acc - mu
            var = jnp.mean(cen * cen, axis=-1, keepdims=True)
            hn = cen * lax.rsqrt(var + EPS) * lng_ref[...] + lnb_ref[...]
            hs_s[r0:r0 + CONV_ROWS, :] = (hn * jax.nn.sigmoid(hn)).astype(BF16)
            xcs = [lbb] * n_grp
            for kk in range(LRU_CONV_WIDTH):
                w8 = lw_ref[_rows(kk), :]
                for gq in range(n_grp):
                    lo = r0 + SUBLANES * (gq + kk)
                    xcs[gq] = xcs[gq] + w8 * lext_s[lo:lo + SUBLANES, :]
            midf_ref[r0:r0 + CONV_ROWS, MID_X:MID_X + MIX_W] = jnp.concatenate(xcs, axis=0)
        midb_ref[:, 0:MIX_W] = (_dot(hs_s[...], wpw_ref[...]) + bpw_ref[...]).astype(BF16)

        c = t % tiles_per_seq
        row = lax.broadcasted_iota(jnp.int32, (T, LANES), 0)
        tok = c * T + (row & (SUBLANES - 1)) * KSUB + (row >> 3)
        base = SUBLANES * p_halo
        for gi, win in enumerate(POOL_WINDOWS):
            cs = slice(LANES * gi, LANES * (gi + 1))
            cur = pext_s[base:base + T, cs]
            ssum = cur
            for dd in range(1, win):
                ssum = ssum + pext_s[base - SUBLANES * dd:base - SUBLANES * dd + T, cs]
            cnt = jnp.minimum(tok + 1, win).astype(F32)
            dg = ssum / cnt - cur
            y = _dot(dg.astype(BF16), pw_ref[gi]) * ps_ref[:, cs]
            midb_ref[:, MIX_W + LANES * gi:MIX_W + LANES * (gi + 1)] = y.astype(BF16)

    @pl.when(s == 0)
    def _():
        cvtail_s[...] = jnp.zeros_like(cvtail_s)
        ltail_s[...] = jnp.zeros_like(ltail_s)
        ptail_s[...] = jnp.zeros_like(ptail_s)
        project()

    @pl.when((s > 0) & (s < n_tiles))
    def _():
        gather(s - 1)
        project()
        mixers(s - 1)

    @pl.when(s == n_tiles)
    def _():
        gather(s - 1)
        mixers(s - 1)


def _back_kernel(midf_ref, midb_ref, x_ref, bw_ref, abar_ref, cw_ref, d_ref, wglu_ref, bglu_ref,
                 wr_ref, br_ref, wi_ref, bi_ref, sp_ref, wout_ref, o_ref,
                 st_s, scar_s, ys5_s, rpre_s, ipre_s, la_s, lh_s, lcar_s, mixed_s,
                 *, n_tiles, tiles_per_seq):
    s = pl.program_id(0)

    def out_proj():
        o_ref[...] = (x_ref[...] + _dot(mixed_s[...], wout_ref[0:2 * MIX_W, :])
                      + _dot(midb_ref[...], wout_ref[2 * MIX_W:4 * MIX_W, :]))

    def mixer_dots():
        for m in range(S5_BLK):
            u = midf_ref[:, MID_U + LANES * m:MID_U + LANES * (m + 1)]
            st_s[:, 1024 * m:1024 * (m + 1)] = _dot(u.astype(BF16), bw_ref[m])
        for hh in range(2):
            cs = slice(256 * hh, 256 * (hh + 1))
            xcb = midf_ref[:, MID_X + 256 * hh:MID_X + 256 * (hh + 1)].astype(BF16)
            rpre_s[:, cs] = _dot(xcb, wr_ref[hh]) + br_ref[:, cs]
            ipre_s[:, cs] = _dot(xcb, wi_ref[hh]) + bi_ref[:, cs]

    def mixer_scans(t):
        first = (t % tiles_per_seq) == 0
        sub = lax.broadcasted_iota(jnp.int32, (SUBLANES, MIX_W), 0)
        keep = jnp.logical_not(first)

        for m in range(S5_BLK):
            rc = slice(1024 * m, 1024 * m + S5_BLK_STATES)
            ic = slice(1024 * m + S5_BLK_STATES, 1024 * (m + 1))
            a_re, a_im = abar_ref[:, rc], abar_ref[:, ic]
            fr = jnp.zeros((SUBLANES, S5_BLK_STATES), F32)
            fi = fr
            for k in range(KSUB):
                tr, ti = _cmul(a_re, a_im, fr, fi)
                fr = tr + st_s[_rows(k), rc]
                fi = ti + st_s[_rows(k), ic]
            pk_re, pk_im = a_re, a_im
            for _ in range(KSUB.bit_length() - 1):
                pk_re, pk_im = _cmul(pk_re, pk_im, pk_re, pk_im)
            xr = jnp.where(sub == 0, jnp.where(keep, _sub_roll(scar_s[:, rc], 1), 0.0), _sub_roll(fr, 1))
            xi = jnp.where(sub == 0, jnp.where(keep, _sub_roll(scar_s[:, ic], 1), 0.0), _sub_roll(fi, 1))
            qr, qi = pk_re, pk_im
            for dlt in (1, 2, 4):
                shr = jnp.where(sub >= dlt, _sub_roll(xr, dlt), 0.0)
                shi = jnp.where(sub >= dlt, _sub_roll(xi, dlt), 0.0)
                tr, ti = _cmul(qr, qi, shr, shi)
                xr, xi = xr + tr, xi + ti
                qr, qi = _cmul(qr, qi, qr, qi)
            tr, ti = _cmul(pk_re, pk_im, xr, xi)
            scar_s[:, rc] = fr + tr
            scar_s[:, ic] = fi + ti
            sr, si = xr, xi
            for k in range(KSUB):
                tr, ti = _cmul(a_re, a_im, sr, si)
                sr = tr + st_s[_rows(k), rc]
                si = ti + st_s[_rows(k), ic]
                st_s[_rows(k), rc] = sr
                st_s[_rows(k), ic] = si
        for m in range(S5_BLK):
            cs = slice(LANES * m, LANES * (m + 1))
            y = _dot(st_s[:, 1024 * m:1024 * (m + 1)].astype(BF16), cw_ref[m])
            ys5_s[:, cs] = y + d_ref[:, cs] * midf_ref[:, MID_U + LANES * m:MID_U + LANES * (m + 1)]
        gl = jax.nn.gelu(ys5_s[...], approximate=True)
        gate = jax.nn.sigmoid(_dot(gl.astype(BF16), wglu_ref[...]) + bglu_ref[...])
        y_s5 = (gl * gate).astype(BF16)

        spb = jnp.broadcast_to(sp_ref[...], (SUBLANES, MIX_W))
        p = jnp.ones((SUBLANES, MIX_W), F32)
        hst = jnp.zeros((SUBLANES, MIX_W), F32)
        for k in range(KSUB):
            r = jax.nn.sigmoid(rpre_s[_rows(k), :])
            ig = jax.nn.sigmoid(ipre_s[_rows(k), :])
            log_a = -LRU_C * r * spb
            a = jnp.exp(log_a)
            th = jnp.tanh(log_a)
            b = jnp.sqrt(-2.0 * th / (1.0 - th)) * (ig * midf_ref[_rows(k), MID_X:MID_X + MIX_W])
            p = a * p
            hst = a * hst + b
            la_s[_rows(k), :] = a
            lh_s[_rows(k), :] = b
        xin = jnp.where(sub == 0, jnp.where(keep, _sub_roll(lcar_s[...], 1), 0.0), _sub_roll(hst, 1))
        mul = _sub_roll(p, 1)
        for dlt in (1, 2, 4):
            xs = jnp.where(sub >= dlt, _sub_roll(xin, dlt), 0.0)
            ms = jnp.where(sub >= dlt, _sub_roll(mul, dlt), 1.0)
            xin = xin + mul * xs
            mul = mul * ms
        lcar_s[...] = hst + p * xin
        hst = xin
        for k in range(KSUB):
            hst = la_s[_rows(k), :] * hst + lh_s[_rows(k), :]
            lh_s[_rows(k), :] = hst * midf_ref[_rows(k), MID_G:MID_G + MIX_W]

        mixed_s[:, 0:MIX_W] = y_s5
        mixed_s[:, MIX_W:2 * MIX_W] = lh_s[...].astype(BF16)

    @pl.when(s == 0)
    def _():
        scar_s[...] = jnp.zeros_like(scar_s)
        lcar_s[...] = jnp.zeros_like(lcar_s)
        mixer_dots()
        mixer_scans(s)

    @pl.when((s > 0) & (s < n_tiles))
    def _():
        mixer_dots()
        out_proj()
        mixer_scans(s)

    @pl.when(s == n_tiles)
    def _():
        out_proj()


def _ffn_kernel(x_ref, g_ref, wg_ref, wv_ref, cw_ref, cb_ref, wd_ref, gf_ref, o_ref,
                h_s, gate0_s, gate1_s, val0_s, val1_s, act_s, tail_s, *, final_norm, tiles_per_seq):
    i = pl.program_id(0)
    j = pl.program_id(1)
    n_sub = FFN_TM // T
    pad = 2 * SUBLANES
    gates = (gate0_s, gate1_s)
    vals = (val0_s, val1_s)

    def up_proj(par):
        hb = h_s[...]
        gates[par][pad:pad + FFN_TM, :] = _dot(hb, wg_ref[...])
        vals[par][...] = _dot(hb, wv_ref[...])

    def down_proj(par, jj):
        g_s = gates[par]
        v_s = vals[par]
        w0 = cw_ref[0:1, :]
        w1 = cw_ref[1:2, :]
        w2 = cw_ref[2:3, :]
        cb = cb_ref[...]
        gc = (cb + w0 * g_s[0:FFN_TM, :] + w1 * g_s[SUBLANES:SUBLANES + FFN_TM, :]
              + w2 * g_s[pad:pad + FFN_TM, :])
        act_s[...] = (jax.nn.gelu(gc, approximate=True) * v_s[...]).astype(BF16)
        sub = lax.broadcasted_iota(jnp.int32, (2, SUBLANES, FFN_TF), 1)
        for s in range(n_sub):
            lo = pad + s * T
            tail = g_s[lo + T - pad:lo + T, :].reshape(2, SUBLANES, FFN_TF)
            prev = tail_s[jj] if s == 0 else g_s[lo - pad:lo, :]
            first = ((i * n_sub + s) % tiles_per_seq) == 0
            prev = jnp.where(first, 0.0, prev).reshape(2, SUBLANES, FFN_TF)
            halo = pltpu.roll(jnp.where(sub == SUBLANES - 1, prev, tail), 1, 1)
            g0 = g_s[lo:lo + SUBLANES, :]
            g1 = g_s[lo + SUBLANES:lo + pad, :]
            fix = jnp.concatenate([cb + w0 * halo[0] + w1 * halo[1] + w2 * g0,
                                   cb + w0 * halo[1] + w1 * g0 + w2 * g1], axis=0)
            act_s[s * T:s * T + pad, :] = (
                jax.nn.gelu(fix, approximate=True) * v_s[s * T:s * T + pad, :]).astype(BF16)
        tail_s[jj] = g_s[FFN_TM:FFN_TM + pad, :]
        ncol = 512
        for n in range(D_MODEL // ncol):
            cs = slice(ncol * n, ncol * (n + 1))
            o_ref[:, cs] = o_ref[:, cs] + _dot(act_s[...], wd_ref[:, cs])

    @pl.when((i == 0) & (j == 0))
    def _():
        tail_s[...] = jnp.zeros_like(tail_s)
        gate0_s[0:pad, :] = jnp.zeros((pad, FFN_TF), F32)
        gate1_s[0:pad, :] = jnp.zeros((pad, FFN_TF), F32)

    @pl.when(j == 0)
    def _():
        x = x_ref[...]
        h_s[...] = _rms(x, g_ref[...]).astype(BF16)
        o_ref[...] = x
        up_proj(0)

    for par in range(2):
        @pl.when((j > 0) & (j < N_FCHUNK) & (j % 2 == par))
        def _(par=par):
            up_proj(par)
            down_proj(1 - par, j - 1)

    @pl.when(j == N_FCHUNK)
    def _():
        down_proj((N_FCHUNK - 1) % 2, N_FCHUNK - 1)
        if final_norm:
            o_ref[...] = _rms(o_ref[...], gf_ref[...])


def _const_spec(shape):
    nd = len(shape)
    return pl.BlockSpec(shape, lambda *_: (0,) * nd, pipeline_mode=pl.Buffered(1))


def _layer_spec(l, shape):
    nd = len(shape)
    return pl.BlockSpec((None,) + tuple(shape), lambda *_: (l,) + (0,) * nd,
                        pipeline_mode=pl.Buffered(1))


def _front_layer(xp, l, w, p, tiles_per_seq):
    n_rows = xp.shape[0]
    n_tiles = n_rows // T
    small = [p["cv_w"], p["cv_b"], p["ln_g"], p["ln_b"], p["w_pw"], p["b_pw"],
             p["l_w"], p["l_b"], p["pool_w"], p["pool_s"]]
    cur = lambda s: (jnp.minimum(s, n_tiles - 1), 0)
    prv = lambda s: (jnp.maximum(s - 1, 0), 0)
    cv_halo = SUBLANES * (CONV_WIDTH - 1)
    l_halo = SUBLANES * (LRU_CONV_WIDTH - 1)
    p_halo = SUBLANES * (max(POOL_WINDOWS) - 1)
    scratch = [
        pltpu.VMEM((T, IN_COLS), F32),
        pltpu.VMEM((cv_halo + T, MIX_W), F32),
        pltpu.VMEM((cv_halo, MIX_W), F32),
        pltpu.VMEM((T, MIX_W), BF16),
        pltpu.VMEM((l_halo + T, MIX_W), F32),
        pltpu.VMEM((l_halo, MIX_W), F32),
        pltpu.VMEM((p_halo + T, MIX_W), F32),
        pltpu.VMEM((p_halo, MIX_W), F32),
    ]
    kern = functools.partial(_front_kernel, n_tiles=n_tiles, tiles_per_seq=tiles_per_seq)
    return pl.pallas_call(
        kern,
        grid=(n_tiles + 1,),
        in_specs=[pl.BlockSpec((T, D_MODEL), cur), _const_spec(p["g"].shape),
                  _layer_spec(l, (D_MODEL, IN_COLS))] + [_const_spec(a.shape) for a in small],
        out_specs=[pl.BlockSpec((T, MIDF_COLS), prv), pl.BlockSpec((T, MIDB_COLS), prv)],
        out_shape=[jax.ShapeDtypeStruct((n_rows, MIDF_COLS), F32),
                   jax.ShapeDtypeStruct((n_rows, MIDB_COLS), BF16)],
        scratch_shapes=scratch,
        compiler_params=pltpu.CompilerParams(
            dimension_semantics=("arbitrary",), vmem_limit_bytes=VMEM_LIMIT),
        name="front_layer",
    )(xp, p["g"], w["w_in"], *small)


def _back_layer(xp, midf, midb, l, w, p, tiles_per_seq):
    n_rows = xp.shape[0]
    n_tiles = n_rows // T
    small = [p["bw"], p["abar"], p["cw"], p["d"], p["w_glu"], p["b_glu"],
             p["w_r"], p["b_r"], p["w_i"], p["b_i"], p["sp"]]
    cur = lambda s: (jnp.minimum(s, n_tiles - 1), 0)
    prv = lambda s: (jnp.maximum(s - 1, 0), 0)
    scratch = [
        pltpu.VMEM((T, S5_COLS), F32),
        pltpu.VMEM((SUBLANES, S5_COLS), F32),
        pltpu.VMEM((T, MIX_W), F32),
        pltpu.VMEM((T, MIX_W), F32),
        pltpu.VMEM((T, MIX_W), F32),
        pltpu.VMEM((T, MIX_W), F32),
        pltpu.VMEM((T, MIX_W), F32),
        pltpu.VMEM((SUBLANES, MIX_W), F32),
        pltpu.VMEM((T, 2 * MIX_W), BF16),
    ]
    kern = functools.partial(_back_kernel, n_tiles=n_tiles, tiles_per_seq=tiles_per_seq)
    return pl.pallas_call(
        kern,
        grid=(n_tiles + 1,),
        in_specs=[pl.BlockSpec((T, MIDF_COLS), cur), pl.BlockSpec((T, MIDB_COLS), prv),
                  pl.BlockSpec((T, D_MODEL), prv)] + [_const_spec(a.shape) for a in small]
                 + [_layer_spec(l, (D_MODEL, D_MODEL))],
        out_specs=pl.BlockSpec((T, D_MODEL), prv),
        out_shape=jax.ShapeDtypeStruct((n_rows, D_MODEL), F32),
        scratch_shapes=scratch,
        compiler_params=pltpu.CompilerParams(
            dimension_semantics=("arbitrary",), vmem_limit_bytes=VMEM_LIMIT),
        name="back_layer",
    )(midf, midb, xp, *small, w["w_out"])


def _ffn_layer(xp, l, w, p, gf, tiles_per_seq, final_norm):
    n_rows = xp.shape[0]
    last = N_FCHUNK - 1
    up_idx = lambda j: jnp.minimum(j, last)
    dn_idx = lambda j: jnp.maximum(j - 1, 0)
    in_specs = [
        pl.BlockSpec((FFN_TM, D_MODEL), lambda i, j: (i, 0), pipeline_mode=pl.Buffered(1)),
        _const_spec((1, D_MODEL)),
        pl.BlockSpec((None, D_MODEL, FFN_TF), lambda i, j: (l, 0, up_idx(j))),
        pl.BlockSpec((None, D_MODEL, FFN_TF), lambda i, j: (l, 0, N_FCHUNK + up_idx(j))),
        pl.BlockSpec((SUBLANES, FFN_TF), lambda i, j: (0, dn_idx(j))),
        pl.BlockSpec((1, FFN_TF), lambda i, j: (0, dn_idx(j))),
        pl.BlockSpec((None, FFN_TF, D_MODEL), lambda i, j: (l, dn_idx(j), 0)),
        _const_spec((1, D_MODEL)),
    ]
    pad = 2 * SUBLANES
    scratch = [
        pltpu.VMEM((FFN_TM, D_MODEL), BF16),
        pltpu.VMEM((pad + FFN_TM, FFN_TF), F32),
        pltpu.VMEM((pad + FFN_TM, FFN_TF), F32),
        pltpu.VMEM((FFN_TM, FFN_TF), F32),
        pltpu.VMEM((FFN_TM, FFN_TF), F32),
        pltpu.VMEM((FFN_TM, FFN_TF), BF16),
        pltpu.VMEM((N_FCHUNK, pad, FFN_TF), F32),
    ]
    kern = functools.partial(_ffn_kernel, final_norm=final_norm, tiles_per_seq=tiles_per_seq)
    return pl.pallas_call(
        kern,
        grid=(n_rows // FFN_TM, N_FCHUNK + 1),
        in_specs=in_specs,
        out_specs=pl.BlockSpec((FFN_TM, D_MODEL), lambda i, j: (i, 0)),
        out_shape=jax.ShapeDtypeStruct((n_rows, D_MODEL), F32),
        scratch_shapes=scratch,
        compiler_params=pltpu.CompilerParams(
            dimension_semantics=("arbitrary", "arbitrary"), vmem_limit_bytes=VMEM_LIMIT),
        name="ffn_final" if final_norm else "ffn_layer",
    )(xp, p["g_ffn"], w["w_up"], w["w_up"], p["f_cw"], p["f_cb"], w["w_down"], gf)


def _state_cols(z):
    zr = jnp.real(z).astype(F32).reshape(S5_BLK, S5_BLK_STATES)
    zi = jnp.imag(z).astype(F32).reshape(S5_BLK, S5_BLK_STATES)
    return jnp.stack([zr, zi], axis=1).reshape(-1)


def _block_diag(w):
    n, a, b = w.shape
    eye = jnp.eye(n, dtype=w.dtype)
    return jnp.einsum("nab,nm->namb", w, eye).reshape(n * a, n * b)


def _prep_big(a):
    w = {}
    w["w_in"] = a["w_in"].astype(BF16)
    wo = a["w_out"]
    w["w_out"] = jnp.concatenate(
        [wo[:, 0:MIX_W], wo[:, 2 * MIX_W:3 * MIX_W], wo[:, MIX_W:2 * MIX_W], wo[:, 3 * MIX_W:]],
        axis=1).astype(BF16)
    depth = wo.shape[0]
    fpad = FFN_PAD - FFN_DIM
    w_up = a["ffn_w_up"]
    zcols = jnp.zeros((depth, D_MODEL, fpad), w_up.dtype)
    w["w_up"] = jnp.concatenate(
        [w_up[:, :, :FFN_DIM], zcols, w_up[:, :, FFN_DIM:], zcols], axis=2).astype(BF16)
    w["w_down"] = jnp.pad(a["ffn_w_down"], ((0, 0), (0, fpad), (0, 0))).astype(BF16)
    return w


def _prep_layer(l, a):
    row = lambda v: v[l].reshape(1, -1).astype(F32)
    p = {}
    p["g"] = row(a["norm_mix_g"])
    lam = lax.complex(a["s5_lam_re"][l].astype(F32), a["s5_lam_im"][l].astype(F32))
    step = jnp.exp(a["s5_log_step"][l].astype(F32))[:, None]
    lam_bar = jnp.exp(lam * step)
    bmat = lax.complex(a["s5_b_re"][l].astype(F32), a["s5_b_im"][l].astype(F32))
    b_bar = ((lam_bar - 1.0) / lam)[..., None] * bmat
    bt = jnp.transpose(b_bar, (0, 2, 1))
    bre = jnp.real(bt).reshape(S5_BLK, 8, S5_GROUP_CH, S5_STATE)
    bim = jnp.imag(bt).reshape(S5_BLK, 8, S5_GROUP_CH, S5_STATE)
    bw = jnp.concatenate([jax.vmap(_block_diag)(bre), jax.vmap(_block_diag)(bim)], axis=-1)
    p["bw"] = bw.astype(BF16)
    p["abar"] = jnp.tile(_state_cols(lam_bar)[None, :], (SUBLANES, 1))
    cre = jnp.transpose(a["s5_c_re"][l].astype(F32), (0, 2, 1)).reshape(S5_BLK, 8, S5_STATE, S5_GROUP_CH)
    cim = jnp.transpose(a["s5_c_im"][l].astype(F32), (0, 2, 1)).reshape(S5_BLK, 8, S5_STATE, S5_GROUP_CH)
    cw = jnp.concatenate([jax.vmap(_block_diag)(cre), -jax.vmap(_block_diag)(cim)], axis=1)
    p["cw"] = cw.astype(BF16)
    p["d"] = row(a["s5_d"])
    p["w_glu"] = a["s5_w_glu"][l].astype(BF16)
    p["b_glu"] = row(a["s5_b_glu"])
    p["cv_w"] = jnp.repeat(a["cv_w_dw"][l].astype(F32), SUBLANES, axis=0)
    p["cv_b"] = row(a["cv_b_dw"])
    p["ln_g"] = row(a["cv_ln_g"])
    p["ln_b"] = row(a["cv_ln_b"])
    p["w_pw"] = a["cv_w_pw"][l].astype(BF16)
    p["b_pw"] = row(a["cv_b_pw"])
    p["l_w"] = jnp.repeat(a["lru_w_conv"][l].astype(F32), SUBLANES, axis=0)
    p["l_b"] = row(a["lru_b_conv"])
    p["w_r"] = jax.vmap(_block_diag)(a["lru_w_r"][l].reshape(2, 4, LRU_HEAD_DIM, LRU_HEAD_DIM)).astype(BF16)
    p["b_r"] = row(a["lru_b_r"])
    p["w_i"] = jax.vmap(_block_diag)(a["lru_w_i"][l].reshape(2, 4, LRU_HEAD_DIM, LRU_HEAD_DIM)).astype(BF16)
    p["b_i"] = row(a["lru_b_i"])
    p["sp"] = jax.nn.softplus(-row(a["lru_lam"]))
    p["pool_w"] = a["pool_w"][l].astype(BF16)
    p["pool_s"] = row(a["pool_scale"])
    p["g_ffn"] = row(a["norm_ffn_g"])
    fpad = FFN_PAD - FFN_DIM
    p["f_cw"] = jnp.pad(a["ffn_w_dw"][l].astype(F32), ((0, SUBLANES - FFN_CONV_WIDTH), (0, fpad)))
    p["f_cb"] = jnp.pad(row(a["ffn_b_dw"]), ((0, 0), (0, fpad)))
    return p


def kernel(x, norm_mix_g, w_in, s5_lam_re, s5_lam_im, s5_log_step, s5_b_re, s5_b_im, s5_c_re, s5_c_im, s5_d, s5_w_glu, s5_b_glu, cv_w_dw, cv_b_dw, cv_ln_g, cv_ln_b, cv_w_pw, cv_b_pw, lru_w_conv, lru_b_conv, lru_w_r, lru_b_r, lru_w_i, lru_b_i, lru_lam, pool_w, pool_scale, w_out, norm_ffn_g, ffn_w_up, ffn_w_dw, ffn_b_dw, ffn_w_down, norm_final_g):
    a = dict(norm_mix_g=norm_mix_g, w_in=w_in, s5_lam_re=s5_lam_re, s5_lam_im=s5_lam_im,
             s5_log_step=s5_log_step, s5_b_re=s5_b_re, s5_b_im=s5_b_im, s5_c_re=s5_c_re,
             s5_c_im=s5_c_im, s5_d=s5_d, s5_w_glu=s5_w_glu, s5_b_glu=s5_b_glu, cv_w_dw=cv_w_dw,
             cv_b_dw=cv_b_dw, cv_ln_g=cv_ln_g, cv_ln_b=cv_ln_b, cv_w_pw=cv_w_pw, cv_b_pw=cv_b_pw,
             lru_w_conv=lru_w_conv, lru_b_conv=lru_b_conv, lru_w_r=lru_w_r, lru_b_r=lru_b_r,
             lru_w_i=lru_w_i, lru_b_i=lru_b_i, lru_lam=lru_lam, pool_w=pool_w,
             pool_scale=pool_scale, w_out=w_out, norm_ffn_g=norm_ffn_g, ffn_w_up=ffn_w_up,
             ffn_w_dw=ffn_w_dw, ffn_b_dw=ffn_b_dw, ffn_w_down=ffn_w_down)
    n_batch, seq, d = x.shape
    depth = w_in.shape[0]
    assert d == D_MODEL and seq % T == 0 and (n_batch * seq) % FFN_TM == 0 and seq % FFN_TM == 0
    tiles_per_seq = seq // T
    xp = x.astype(F32).reshape(n_batch, tiles_per_seq, SUBLANES, KSUB, d)
    xp = jnp.transpose(xp, (0, 1, 3, 2, 4)).reshape(n_batch * seq, d)
    gf = norm_final_g.reshape(1, -1).astype(F32)
    w = _prep_big(a)
    for l in range(depth):
        p = _prep_layer(l, a)
        midf, midb = _front_layer(xp, l, w, p, tiles_per_seq)
        xp = _back_layer(xp, midf, midb, l, w, p, tiles_per_seq)
        xp = _ffn_layer(xp, l, w, p, gf, tiles_per_seq, final_norm=(l == depth - 1))
    out = xp.reshape(n_batch, tiles_per_seq, KSUB, SUBLANES, d)
    out = jnp.transpose(out, (0, 1, 3, 2, 4)).reshape(n_batch, seq, d)
    return out.astype(x.dtype)
```

```python
import functools

import jax
import jax.numpy as jnp
from jax import lax
from jax.experimental import pallas as pl
from jax.experimental.pallas import tpu as pltpu

D_MODEL = 2048
MIX_W = 512
IN_COLS = 6 * MIX_W
S5_GROUP_CH = 16
S5_STATE = 64
CONV_WIDTH = 31
LRU_HEADS = 8
LRU_HEAD_DIM = MIX_W // LRU_HEADS
LRU_CONV_WIDTH = 4
LRU_C = 8.0
POOL_WINDOWS = (2, 4, 8, 16)
FFN_DIM = 5504
FFN_CONV_WIDTH = 3
EPS = 1e-6

SUBLANES = 8
LANES = 128

T = 256
KSUB = T // SUBLANES
FFN_TM = 1024
FFN_TF = 512
FFN_PAD = 5632
N_FCHUNK = FFN_PAD // FFN_TF
S5_BLK = 4
S5_BLK_STATES = 512
S5_COLS = 2 * S5_BLK * S5_BLK_STATES
CONV_ROWS = 32
VMEM_LIMIT = 56 * 1024 * 1024
FFN_VMEM_LIMIT = VMEM_LIMIT

MID_U, MID_X, MID_G = 0, MIX_W, 2 * MIX_W
MIDF_COLS = 3 * MIX_W
MIDB_COLS = 2 * MIX_W

F32 = jnp.float32
BF16 = jnp.bfloat16


def _dot(a, b):
    return jnp.dot(a, b, preferred_element_type=F32)


def _rms(x, g):
    ms = jnp.mean(x * x, axis=-1, keepdims=True)
    return x * lax.rsqrt(ms + EPS) * g


def _cmul(ar, ai, br, bi):
    return ar * br - ai * bi, ar * bi + ai * br


def _sub_roll(x, d):
    return pltpu.roll(x, d, 0)


def _rows(k):
    return slice(SUBLANES * k, SUBLANES * (k + 1))


def _fill_ext(ext_ref, tail_ref, a, dv, first):
    w = a.shape[1]
    n = SUBLANES * dv
    ext_ref[n:n + T, :] = a
    tail = a[T - n:, :].reshape(dv, SUBLANES, w)
    prev = jnp.where(first, 0.0, tail_ref[...]).reshape(dv, SUBLANES, w)
    sub = lax.broadcasted_iota(jnp.int32, (dv, SUBLANES, w), 1)
    halo = pltpu.roll(jnp.where(sub == SUBLANES - 1, prev, tail), 1, 1)
    ext_ref[0:n, :] = halo.reshape(n, w)
    tail_ref[...] = a[T - n:, :]


def _front_kernel(x_ref, g_ref, win_ref, cvw_ref, cvb_ref, lng_ref, lnb_ref, wpw_ref, bpw_ref,
                  lw_ref, lb_ref, pw_ref, ps_ref,
                  midf_ref, midb_ref,
                  proj_s, cvext_s, cvtail_s, hs_s, lext_s, ltail_s, pext_s, ptail_s,
                  *, n_tiles, tiles_per_seq):
    s = pl.program_id(0)
    cv_halo = CONV_WIDTH - 1
    l_halo = LRU_CONV_WIDTH - 1
    p_halo = max(POOL_WINDOWS) - 1
    n_grp = CONV_ROWS // SUBLANES

    def project():
        h = _rms(x_ref[...], g_ref[...]).astype(BF16)
        proj_s[...] = _dot(h, win_ref[...])

    def gather(t):
        first = (t % tiles_per_seq) == 0
        midf_ref[:, MID_U:MID_U + MIX_W] = proj_s[:, 0:MIX_W]
        hc = proj_s[:, MIX_W:2 * MIX_W] * jax.nn.sigmoid(proj_s[:, 2 * MIX_W:3 * MIX_W])
        _fill_ext(cvext_s, cvtail_s, hc, cv_halo, first)
        _fill_ext(lext_s, ltail_s, proj_s[:, 3 * MIX_W:4 * MIX_W], l_halo, first)
        midf_ref[:, MID_G:MID_G + MIX_W] = jax.nn.gelu(proj_s[:, 4 * MIX_W:5 * MIX_W], approximate=True)
        _fill_ext(pext_s, ptail_s, proj_s[:, 5 * MIX_W:6 * MIX_W], p_halo, first)

    def mixers(t):
        cvb = jnp.broadcast_to(cvb_ref[...], (SUBLANES, MIX_W))
        lbb = jnp.broadcast_to(lb_ref[...], (SUBLANES, MIX_W))
        for rb in range(T // CONV_ROWS):
            r0 = rb * CONV_ROWS
            accs = [cvb] * n_grp
            for kk in range(CONV_WIDTH):
                w8 = cvw_ref[_rows(kk), :]
                for gq in range(n_grp):
                    lo = r0 + SUBLANES * (gq + kk)
                    accs[gq] = accs[gq] + w8 * cvext_s[lo:lo + SUBLANES, :]
            acc = jnp.concatenate(accs, axis=0)
            mu = jnp.mean(acc, axis=-1, keepdims=True)
            cen = acc - mu
            var = jnp.mean(cen * cen, axis=-1, keepdims=True)
            hn = cen * lax.rsqrt(var + EPS) * lng_ref[...] + lnb_ref[...]
            hs_s[r0:r0 + CONV_ROWS, :] = (hn * jax.nn.sigmoid(hn)).astype(BF16)
            xcs = [lbb] * n_grp
            for kk in range(LRU_CONV_WIDTH):
                w8 = lw_ref[_rows(kk), :]
                for gq in range(n_grp):
                    lo = r0 + SUBLANES * (gq + kk)
                    xcs[gq] = xcs[gq] + w8 * lext_s[lo:lo + SUBLANES, :]
            midf_ref[r0:r0 + CONV_ROWS, MID_X:MID_X + MIX_W] = jnp.concatenate(xcs, axis=0)
        midb_ref[:, 0:MIX_W] = (_dot(hs_s[...], wpw_ref[...]) + bpw_ref[...]).astype(BF16)

        c = t % tiles_per_seq
        row = lax.broadcasted_iota(jnp.int32, (T, LANES), 0)
        tok = c * T + (row & (SUBLANES - 1)) * KSUB + (row >> 3)
        base = SUBLANES * p_halo
        for gi, win in enumerate(POOL_WINDOWS):
            cs = slice(LANES * gi, LANES * (gi + 1))
            cur = pext_s[base:base + T, cs]
            ssum = cur
            for dd in range(1, win):
                ssum = ssum + pext_s[base - SUBLANES * dd:base - SUBLANES * dd + T, cs]
            cnt = jnp.minimum(tok + 1, win).astype(F32)
            dg = ssum / cnt - cur
            y = _dot(dg.astype(BF16), pw_ref[gi]) * ps_ref[:, cs]
            midb_ref[:, MIX_W + LANES * gi:MIX_W + LANES * (gi + 1)] = y.astype(BF16)

    @pl.when(s == 0)
    def _():
        cvtail_s[...] = jnp.zeros_like(cvtail_s)
        ltail_s[...] = jnp.zeros_like(ltail_s)
        ptail_s[...] = jnp.zeros_like(ptail_s)
        project()

    @pl.when((s > 0) & (s < n_tiles))
    def _():
        gather(s - 1)
        project()
        mixers(s - 1)

    @pl.when(s == n_tiles)
    def _():
        gather(s - 1)
        mixers(s - 1)


def _back_kernel(midf_ref, midb_ref, x_ref, bw_ref, abar_ref, cw_ref, d_ref, wglu_ref, bglu_ref,
                 wr_ref, br_ref, wi_ref, bi_ref, sp_ref, wout_ref, o_ref,
                 st_s, scar_s, ys5_s, rpre_s, ipre_s, la_s, lh_s, lcar_s, mixed_s,
                 *, n_tiles, tiles_per_seq):
    s = pl.program_id(0)

    def out_proj():
        o_ref[...] = (x_ref[...] + _dot(mixed_s[...], wout_ref[0:2 * MIX_W, :])
                      + _dot(midb_ref[...], wout_ref[2 * MIX_W:4 * MIX_W, :]))

    def mixer_dots():
        for m in range(S5_BLK):
            u = midf_ref[:, MID_U + LANES * m:MID_U + LANES * (m + 1)]
            st_s[:, 1024 * m:1024 * (m + 1)] = _dot(u.astype(BF16), bw_ref[m])
        for hh in range(2):
            cs = slice(256 * hh, 256 * (hh + 1))
            xcb = midf_ref[:, MID_X + 256 * hh:MID_X + 256 * (hh + 1)].astype(BF16)
            rpre_s[:, cs] = _dot(xcb, wr_ref[hh]) + br_ref[:, cs]
            ipre_s[:, cs] = _dot(xcb, wi_ref[hh]) + bi_ref[:, cs]

    def mixer_scans(t):
        first = (t % tiles_per_seq) == 0
        sub = lax.broadcasted_iota(jnp.int32, (SUBLANES, MIX_W), 0)
        keep = jnp.logical_not(first)

        for m in range(S5_BLK):
            rc = slice(1024 * m, 1024 * m + S5_BLK_STATES)
            ic = slice(1024 * m + S5_BLK_STATES, 1024 * (m + 1))
            a_re, a_im = abar_ref[:, rc], abar_ref[:, ic]
            fr = jnp.zeros((SUBLANES, S5_BLK_STATES), F32)
            fi = fr
            for k in range(KSUB):
                tr, ti = _cmul(a_re, a_im, fr, fi)
                fr = tr + st_s[_rows(k), rc]
                fi = ti + st_s[_rows(k), ic]
            pk_re, pk_im = a_re, a_im
            for _ in range(KSUB.bit_length() - 1):
                pk_re, pk_im = _cmul(pk_re, pk_im, pk_re, pk_im)
            xr = jnp.where(sub == 0, jnp.where(keep, _sub_roll(scar_s[:, rc], 1), 0.0), _sub_roll(fr, 1))
            xi = jnp.where(sub == 0, jnp.where(keep, _sub_roll(scar_s[:, ic], 1), 0.0), _sub_roll(fi, 1))
            qr, qi = pk_re, pk_im
            for dlt in (1, 2, 4):
                shr = jnp.where(sub >= dlt, _sub_roll(xr, dlt), 0.0)
                shi = jnp.where(sub >= dlt, _sub_roll(xi, dlt), 0.0)
                tr, ti = _cmul(qr, qi, shr, shi)
                xr, xi = xr + tr, xi + ti
                qr, qi = _cmul(qr, qi, qr, qi)
            tr, ti = _cmul(pk_re, pk_im, xr, xi)
            scar_s[:, rc] = fr + tr
            scar_s[:, ic] = fi + ti
            sr, si = xr, xi
            for k in range(KSUB):
                tr, ti = _cmul(a_re, a_im, sr, si)
                sr = tr + st_s[_rows(k), rc]
                si = ti + st_s[_rows(k), ic]
                st_s[_rows(k), rc] = sr
                st_s[_rows(k), ic] = si
        for m in range(S5_BLK):
            cs = slice(LANES * m, LANES * (m + 1))
            y = _dot(st_s[:, 1024 * m:1024 * (m + 1)].astype(BF16), cw_ref[m])
            ys5_s[:, cs] = y + d_ref[:, cs] * midf_ref[:, MID_U + LANES * m:MID_U + LANES * (m + 1)]
        gl = jax.nn.gelu(ys5_s[...], approximate=True)
        gate = jax.nn.sigmoid(_dot(gl.astype(BF16), wglu_ref[...]) + bglu_ref[...])
        y_s5 = (gl * gate).astype(BF16)

        spb = jnp.broadcast_to(sp_ref[...], (SUBLANES, MIX_W))
        p = jnp.ones((SUBLANES, MIX_W), F32)
        hst = jnp.zeros((SUBLANES, MIX_W), F32)
        for k in range(KSUB):
            r = jax.nn.sigmoid(rpre_s[_rows(k), :])
            ig = jax.nn.sigmoid(ipre_s[_rows(k), :])
            log_a = -LRU_C * r * spb
            a = jnp.exp(log_a)
            th = jnp.tanh(log_a)
            b = jnp.sqrt(-2.0 * th / (1.0 - th)) * (ig * midf_ref[_rows(k), MID_X:MID_X + MIX_W])
            p = a * p
            hst = a * hst + b
            la_s[_rows(k), :] = a
            lh_s[_rows(k), :] = b
        xin = jnp.where(sub == 0, jnp.where(keep, _sub_roll(lcar_s[...], 1), 0.0), _sub_roll(hst, 1))
        mul = _sub_roll(p, 1)
        for dlt in (1, 2, 4):
            xs = jnp.where(sub >= dlt, _sub_roll(xin, dlt), 0.0)
            ms = jnp.where(sub >= dlt, _sub_roll(mul, dlt), 1.0)
            xin = xin + mul * xs
            mul = mul * ms
        lcar_s[...] = hst + p * xin
        hst = xin
        for k in range(KSUB):
            hst = la_s[_rows(k), :] * hst + lh_s[_rows(k), :]
            lh_s[_rows(k), :] = hst * midf_ref[_rows(k), MID_G:MID_G + MIX_W]

        mixed_s[:, 0:MIX_W] = y_s5
        mixed_s[:, MIX_W:2 * MIX_W] = lh_s[...].astype(BF16)

    @pl.when(s == 0)
    def _():
        scar_s[...] = jnp.zeros_like(scar_s)
        lcar_s[...] = jnp.zeros_like(lcar_s)
        mixer_dots()
        mixer_scans(s)

    @pl.when((s > 0) & (s < n_tiles))
    def _():
        mixer_dots()
        out_proj()
        mixer_scans(s)

    @pl.when(s == n_tiles)
    def _():
        out_proj()


def _ffn_kernel(x_ref, g_ref, wg_ref, wv_ref, cw_ref, cb_ref, wd_ref, gf_ref, o_ref,
                h_s, gate0_s, gate1_s, val0_s, val1_s, act_s, tail_s, *, final_norm, tiles_per_seq):
    i = pl.program_id(0)
    j = pl.program_id(1)
    n_sub = FFN_TM // T
    pad = 2 * SUBLANES
    gates = (gate0_s, gate1_s)
    vals = (val0_s, val1_s)

    def up_proj(par):
        hb = h_s[...]
        gates[par][pad:pad + FFN_TM, :] = _dot(hb, wg_ref[...])
        vals[par][...] = _dot(hb, wv_ref[...])

    def down_proj(par, jj):
        g_s = gates[par]
        v_s = vals[par]
        w0 = cw_ref[0:1, :]
        w1 = cw_ref[1:2, :]
        w2 = cw_ref[2:3, :]
        cb = cb_ref[...]
        gc = (cb + w0 * g_s[0:FFN_TM, :] + w1 * g_s[SUBLANES:SUBLANES + FFN_TM, :]
              + w2 * g_s[pad:pad + FFN_TM, :])
        act_s[...] = (jax.nn.gelu(gc, approximate=True) * v_s[...]).astype(BF16)
        sub = lax.broadcasted_iota(jnp.int32, (2, SUBLANES, FFN_TF), 1)
        for s in range(n_sub):
            lo = pad + s * T
            tail = g_s[lo + T - pad:lo + T, :].reshape(2, SUBLANES, FFN_TF)
            prev = tail_s[jj] if s == 0 else g_s[lo - pad:lo, :]
            first = ((i * n_sub + s) % tiles_per_seq) == 0
            prev = jnp.where(first, 0.0, prev).reshape(2, SUBLANES, FFN_TF)
            halo = pltpu.roll(jnp.where(sub == SUBLANES - 1, prev, tail), 1, 1)
            g0 = g_s[lo:lo + SUBLANES, :]
            g1 = g_s[lo + SUBLANES:lo + pad, :]
            fix = jnp.concatenate([cb + w0 * halo[0] + w1 * halo[1] + w2 * g0,
                                   cb + w0 * halo[1] + w1 * g0 + w2 * g1], axis=0)
            act_s[s * T:s * T + pad, :] = (
                jax.nn.gelu(fix, approximate=True) * v_s[s * T:s * T + pad, :]).astype(BF16)
        tail_s[jj] = g_s[FFN_TM:FFN_TM + pad, :]
        ncol = 512
        for n in range(D_MODEL // ncol):
            cs = slice(ncol * n, ncol * (n + 1))
            o_ref[:, cs] = o_ref[:, cs] + _dot(act_s[...], wd_ref[:, cs])

    @pl.when((i == 0) & (j == 0))
    def _():
        tail_s[...] = jnp.zeros_like(tail_s)
        gate0_s[0:pad, :] = jnp.zeros((pad, FFN_TF), F32)
        gate1_s[0:pad, :] = jnp.zeros((pad, FFN_TF), F32)

    @pl.when(j == 0)
    def _():
        x = x_ref[...]
        h_s[...] = _rms(x, g_ref[...]).astype(BF16)
        o_ref[...] = x
        up_proj(0)

    for par in range(2):
        @pl.when((j > 0) & (j < N_FCHUNK) & (j % 2 == par))
        def _(par=par):
            up_proj(par)
            down_proj(1 - par, j - 1)

    @pl.when(j == N_FCHUNK)
    def _():
        down_proj((N_FCHUNK - 1) % 2, N_FCHUNK - 1)
        if final_norm:
            o_ref[...] = _rms(o_ref[...], gf_ref[...])


def _const_spec(shape):
    nd = len(shape)
    return pl.BlockSpec(shape, lambda *_: (0,) * nd, pipeline_mode=pl.Buffered(1))


def _layer_spec(l, shape):
    nd = len(shape)
    return pl.BlockSpec((None,) + tuple(shape), lambda *_: (l,) + (0,) * nd,
                        pipeline_mode=pl.Buffered(1))


def _front_layer(xp, l, w, p, tiles_per_seq):
    n_rows = xp.shape[0]
    n_tiles = n_rows // T
    small = [p["cv_w"], p["cv_b"], p["ln_g"], p["ln_b"], p["w_pw"], p["b_pw"],
             p["l_w"], p["l_b"], p["pool_w"], p["pool_s"]]
    cur = lambda s: (jnp.minimum(s, n_tiles - 1), 0)
    prv = lambda s: (jnp.maximum(s - 1, 0), 0)
    cv_halo = SUBLANES * (CONV_WIDTH - 1)
    l_halo = SUBLANES * (LRU_CONV_WIDTH - 1)
    p_halo = SUBLANES * (max(POOL_WINDOWS) - 1)
    scratch = [
        pltpu.VMEM((T, IN_COLS), F32),
        pltpu.VMEM((cv_halo + T, MIX_W), F32),
        pltpu.VMEM((cv_halo, MIX_W), F32),
        pltpu.VMEM((T, MIX_W), BF16),
        pltpu.VMEM((l_halo + T, MIX_W), F32),
        pltpu.VMEM((l_halo, MIX_W), F32),
        pltpu.VMEM((p_halo + T, MIX_W), F32),
        pltpu.VMEM((p_halo, MIX_W), F32),
    ]
    kern = functools.partial(_front_kernel, n_tiles=n_tiles, tiles_per_seq=tiles_per_seq)
    return pl.pallas_call(
        kern,
        grid=(n_tiles + 1,),
        in_specs=[pl.BlockSpec((T, D_MODEL), cur), _const_spec(p["g"].shape),
                  _layer_spec(l, (D_MODEL, IN_COLS))] + [_const_spec(a.shape) for a in small],
        out_specs=[pl.BlockSpec((T, MIDF_COLS), prv), pl.BlockSpec((T, MIDB_COLS), prv)],
        out_shape=[jax.ShapeDtypeStruct((n_rows, MIDF_COLS), F32),
                   jax.ShapeDtypeStruct((n_rows, MIDB_COLS), BF16)],
        scratch_shapes=scratch,
        compiler_params=pltpu.CompilerParams(
            dimension_semantics=("arbitrary",), vmem_limit_bytes=VMEM_LIMIT),
        name="front_layer",
    )(xp, p["g"], w["w_in"], *small)


def _back_layer(xp, midf, midb, l, w, p, tiles_per_seq):
    n_rows = xp.shape[0]
    n_tiles = n_rows // T
    small = [p["bw"], p["abar"], p["cw"], p["d"], p["w_glu"], p["b_glu"],
             p["w_r"], p["b_r"], p["w_i"], p["b_i"], p["sp"]]
    cur = lambda s: (jnp.minimum(s, n_tiles - 1), 0)
    prv = lambda s: (jnp.maximum(s - 1, 0), 0)
    scratch = [
        pltpu.VMEM((T, S5_COLS), F32),
        pltpu.VMEM((SUBLANES, S5_COLS), F32),
        pltpu.VMEM((T, MIX_W), F32),
        pltpu.VMEM((T, MIX_W), F32),
        pltpu.VMEM((T, MIX_W), F32),
        pltpu.VMEM((T, MIX_W), F32),
        pltpu.VMEM((T, MIX_W), F32),
        pltpu.VMEM((SUBLANES, MIX_W), F32),
        pltpu.VMEM((T, 2 * MIX_W), BF16),
    ]
    kern = functools.partial(_back_kernel, n_tiles=n_tiles, tiles_per_seq=tiles_per_seq)
    return pl.pallas_call(
        kern,
        grid=(n_tiles + 1,),
        in_specs=[pl.BlockSpec((T, MIDF_COLS), cur), pl.BlockSpec((T, MIDB_COLS), prv),
                  pl.BlockSpec((T, D_MODEL), prv)] + [_const_spec(a.shape) for a in small]
                 + [_layer_spec(l, (D_MODEL, D_MODEL))],
        out_specs=pl.BlockSpec((T, D_MODEL), prv),
        out_shape=jax.ShapeDtypeStruct((n_rows, D_MODEL), F32),
        scratch_shapes=scratch,
        compiler_params=pltpu.CompilerParams(
            dimension_semantics=("arbitrary",), vmem_limit_bytes=VMEM_LIMIT),
        name="back_layer",
    )(midf, midb, xp, *small, w["w_out"])


def _ffn_layer(xp, l, w, p, gf, tiles_per_seq, final_norm):
    n_rows = xp.shape[0]
    last = N_FCHUNK - 1
    up_idx = lambda j: jnp.minimum(j, last)
    dn_idx = lambda j: jnp.maximum(j - 1, 0)
    in_specs = [
        pl.BlockSpec((FFN_TM, D_MODEL), lambda i, j: (i, 0), pipeline_mode=pl.Buffered(1)),
        _const_spec((1, D_MODEL)),
        pl.BlockSpec((None, D_MODEL, FFN_TF), lambda i, j: (l, 0, up_idx(j))),
        pl.BlockSpec((None, D_MODEL, FFN_TF), lambda i, j: (l, 0, N_FCHUNK + up_idx(j))),
        pl.BlockSpec((SUBLANES, FFN_TF), lambda i, j: (0, dn_idx(j))),
        pl.BlockSpec((1, FFN_TF), lambda i, j: (0, dn_idx(j))),
        pl.BlockSpec((None, FFN_TF, D_MODEL), lambda i, j: (l, dn_idx(j), 0)),
        _const_spec((1, D_MODEL)),
    ]
    pad = 2 * SUBLANES
    scratch = [
        pltpu.VMEM((FFN_TM, D_MODEL), BF16),
        pltpu.VMEM((pad + FFN_TM, FFN_TF), F32),
        pltpu.VMEM((pad + FFN_TM, FFN_TF), F32),
        pltpu.VMEM((FFN_TM, FFN_TF), F32),
        pltpu.VMEM((FFN_TM, FFN_TF), F32),
        pltpu.VMEM((FFN_TM, FFN_TF), BF16),
        pltpu.VMEM((N_FCHUNK, pad, FFN_TF), F32),
    ]
    kern = functools.partial(_ffn_kernel, final_norm=final_norm, tiles_per_seq=tiles_per_seq)
    return pl.pallas_call(
        kern,
        grid=(n_rows // FFN_TM, N_FCHUNK + 1),
        in_specs=in_specs,
        out_specs=pl.BlockSpec((FFN_TM, D_MODEL), lambda i, j: (i, 0)),
        out_shape=jax.ShapeDtypeStruct((n_rows, D_MODEL), F32),
        scratch_shapes=scratch,
        compiler_params=pltpu.CompilerParams(
            dimension_semantics=("arbitrary", "arbitrary"), vmem_limit_bytes=FFN_VMEM_LIMIT),
        name="ffn_final" if final_norm else "ffn_layer",
    )(xp, p["g_ffn"], w["w_up"], w["w_up"], p["f_cw"], p["f_cb"], w["w_down"], gf)


def _state_cols(z):
    zr = jnp.real(z).astype(F32).reshape(S5_BLK, S5_BLK_STATES)
    zi = jnp.imag(z).astype(F32).reshape(S5_BLK, S5_BLK_STATES)
    return jnp.stack([zr, zi], axis=1).reshape(-1)


def _block_diag(w):
    n, a, b = w.shape
    eye = jnp.eye(n, dtype=w.dtype)
    return jnp.einsum("nab,nm->namb", w, eye).reshape(n * a, n * b)


def _cast_kernel(x_ref, o_ref, *, valid_fn):
    rows_ok, cols_ok = valid_fn(pl.program_id(1))
    x = x_ref[...].reshape(o_ref.shape)
    if rows_ok is not None:
        x = jnp.where(lax.broadcasted_iota(jnp.int32, x.shape, 0) < rows_ok, x, 0.0)
    if cols_ok is not None:
        x = jnp.where(lax.broadcasted_iota(jnp.int32, x.shape, 1) < cols_ok, x, 0.0)
    o_ref[...] = x.astype(BF16)


def _cast_weight(x, out_shape, in_block, in_index, out_block, n_blocks, valid_fn, name):
    return pl.pallas_call(
        functools.partial(_cast_kernel, valid_fn=valid_fn),
        grid=(x.shape[0], n_blocks),
        in_specs=[pl.BlockSpec(in_block, in_index)],
        out_specs=pl.BlockSpec(out_block[0], out_block[1]),
        out_shape=jax.ShapeDtypeStruct(out_shape, BF16),
        compiler_params=pltpu.CompilerParams(
            dimension_semantics=("arbitrary", "arbitrary"), vmem_limit_bytes=VMEM_LIMIT),
        name=name,
    )(x)


def _prep_big(a):
    w = {}
    depth = a["w_in"].shape[0]
    no_mask = lambda b: (None, None)
    cb = 512
    w["w_in"] = _cast_weight(
        a["w_in"], (depth, D_MODEL, IN_COLS), (None, D_MODEL, cb), lambda l, b: (l, 0, b),
        ((None, D_MODEL, cb), lambda l, b: (l, 0, b)), IN_COLS // cb, no_mask, "cast_w_in")
    swap = lambda b: b + (b == 1).astype(jnp.int32) - (b == 2).astype(jnp.int32)
    w["w_out"] = _cast_weight(
        a["w_out"], (depth, D_MODEL, D_MODEL), (None, MIX_W, D_MODEL), lambda l, b: (l, swap(b), 0),
        ((None, MIX_W, D_MODEL), lambda l, b: (l, b, 0)), D_MODEL // MIX_W, no_mask, "cast_w_out")
    lanes_tf, lanes_f = FFN_TF // LANES, FFN_DIM // LANES
    up_off = lambda b: LANES * jnp.where(b < N_FCHUNK, lanes_tf * b, lanes_f + lanes_tf * (b - N_FCHUNK))
    up_valid = lambda b: (None, jnp.where(b < N_FCHUNK, FFN_DIM, 2 * FFN_DIM) - up_off(b))
    w["w_up"] = _cast_weight(
        a["ffn_w_up"], (depth, D_MODEL, 2 * FFN_PAD),
        (pl.Element(1), pl.Element(D_MODEL), pl.Element(FFN_TF)), lambda l, b: (l, 0, up_off(b)),
        ((None, D_MODEL, FFN_TF), lambda l, b: (l, 0, b)), 2 * N_FCHUNK, up_valid, "cast_w_up")
    w["w_down"] = _cast_weight(
        a["ffn_w_down"], (depth, FFN_PAD, D_MODEL), (None, FFN_TF, D_MODEL), lambda l, b: (l, b, 0),
        ((None, FFN_TF, D_MODEL), lambda l, b: (l, b, 0)), N_FCHUNK,
        lambda b: (FFN_DIM - FFN_TF * b, None), "cast_w_down")
    return w


def _prep_layer(l, a):
    row = lambda v: v[l].reshape(1, -1).astype(F32)
    p = {}
    p["g"] = row(a["norm_mix_g"])
    lam = lax.complex(a["s5_lam_re"][l].astype(F32), a["s5_lam_im"][l].astype(F32))
    step = jnp.exp(a["s5_log_step"][l].astype(F32))[:, None]
    lam_bar = jnp.exp(lam * step)
    bmat = lax.complex(a["s5_b_re"][l].astype(F32), a["s5_b_im"][l].astype(F32))
    b_bar = ((lam_bar - 1.0) / lam)[..., None] * bmat
    bt = jnp.transpose(b_bar, (0, 2, 1))
    bre = jnp.real(bt).reshape(S5_BLK, 8, S5_GROUP_CH, S5_STATE)
    bim = jnp.imag(bt).reshape(S5_BLK, 8, S5_GROUP_CH, S5_STATE)
    bw = jnp.concatenate([jax.vmap(_block_diag)(bre), jax.vmap(_block_diag)(bim)], axis=-1)
    p["bw"] = bw.astype(BF16)
    p["abar"] = jnp.tile(_state_cols(lam_bar)[None, :], (SUBLANES, 1))
    cre = jnp.transpose(a["s5_c_re"][l].astype(F32), (0, 2, 1)).reshape(S5_BLK, 8, S5_STATE, S5_GROUP_CH)
    cim = jnp.transpose(a["s5_c_im"][l].astype(F32), (0, 2, 1)).reshape(S5_BLK, 8, S5_STATE, S5_GROUP_CH)
    cw = jnp.concatenate([jax.vmap(_block_diag)(cre), -jax.vmap(_block_diag)(cim)], axis=1)
    p["cw"] = cw.astype(BF16)
    p["d"] = row(a["s5_d"])
    p["w_glu"] = a["s5_w_glu"][l].astype(BF16)
    p["b_glu"] = row(a["s5_b_glu"])
    p["cv_w"] = jnp.repeat(a["cv_w_dw"][l].astype(F32), SUBLANES, axis=0)
    p["cv_b"] = row(a["cv_b_dw"])
    p["ln_g"] = row(a["cv_ln_g"])
    p["ln_b"] = row(a["cv_ln_b"])
    p["w_pw"] = a["cv_w_pw"][l].astype(BF16)
    p["b_pw"] = row(a["cv_b_pw"])
    p["l_w"] = jnp.repeat(a["lru_w_conv"][l].astype(F32), SUBLANES, axis=0)
    p["l_b"] = row(a["lru_b_conv"])
    p["w_r"] = jax.vmap(_block_diag)(a["lru_w_r"][l].reshape(2, 4, LRU_HEAD_DIM, LRU_HEAD_DIM)).astype(BF16)
    p["b_r"] = row(a["lru_b_r"])
    p["w_i"] = jax.vmap(_block_diag)(a["lru_w_i"][l].reshape(2, 4, LRU_HEAD_DIM, LRU_HEAD_DIM)).astype(BF16)
    p["b_i"] = row(a["lru_b_i"])
    p["sp"] = jax.nn.softplus(-row(a["lru_lam"]))
    p["pool_w"] = a["pool_w"][l].astype(BF16)
    p["pool_s"] = row(a["pool_scale"])
    p["g_ffn"] = row(a["norm_ffn_g"])
    fpad = FFN_PAD - FFN_DIM
    p["f_cw"] = jnp.pad(a["ffn_w_dw"][l].astype(F32), ((0, SUBLANES - FFN_CONV_WIDTH), (0, fpad)))
    p["f_cb"] = jnp.pad(row(a["ffn_b_dw"]), ((0, 0), (0, fpad)))
    return p


def kernel(x, norm_mix_g, w_in, s5_lam_re, s5_lam_im, s5_log_step, s5_b_re, s5_b_im, s5_c_re, s5_c_im, s5_d, s5_w_glu, s5_b_glu, cv_w_dw, cv_b_dw, cv_ln_g, cv_ln_b, cv_w_pw, cv_b_pw, lru_w_conv, lru_b_conv, lru_w_r, lru_b_r, lru_w_i, lru_b_i, lru_lam, pool_w, pool_scale, w_out, norm_ffn_g, ffn_w_up, ffn_w_dw, ffn_b_dw, ffn_w_down, norm_final_g):
    a = dict(norm_mix_g=norm_mix_g, w_in=w_in, s5_lam_re=s5_lam_re, s5_lam_im=s5_lam_im,
             s5_log_step=s5_log_step, s5_b_re=s5_b_re, s5_b_im=s5_b_im, s5_c_re=s5_c_re,
             s5_c_im=s5_c_im, s5_d=s5_d, s5_w_glu=s5_w_glu, s5_b_glu=s5_b_glu, cv_w_dw=cv_w_dw,
             cv_b_dw=cv_b_dw, cv_ln_g=cv_ln_g, cv_ln_b=cv_ln_b, cv_w_pw=cv_w_pw, cv_b_pw=cv_b_pw,
             lru_w_conv=lru_w_conv, lru_b_conv=lru_b_conv, lru_w_r=lru_w_r, lru_b_r=lru_b_r,
             lru_w_i=lru_w_i, lru_b_i=lru_b_i, lru_lam=lru_lam, pool_w=pool_w,
             pool_scale=pool_scale, w_out=w_out, norm_ffn_g=norm_ffn_g, ffn_w_up=ffn_w_up,
             ffn_w_dw=ffn_w_dw, ffn_b_dw=ffn_b_dw, ffn_w_down=ffn_w_down)
    n_batch, seq, d = x.shape
    depth = w_in.shape[0]
    assert d == D_MODEL and seq % T == 0 and (n_batch * seq) % FFN_TM == 0 and seq % FFN_TM == 0
    tiles_per_seq = seq // T
    xp = x.astype(F32).reshape(n_batch, tiles_per_seq, SUBLANES, KSUB, d)
    xp = jnp.transpose(xp, (0, 1, 3, 2, 4)).reshape(n_batch * seq, d)
    gf = norm_final_g.reshape(1, -1).astype(F32)
    w = _prep_big(a)
    for l in range(depth):
        p = _prep_layer(l, a)
        midf, midb = _front_layer(xp, l, w, p, tiles_per_seq)
        xp = _back_layer(xp, midf, midb, l, w, p, tiles_per_seq)
        xp = _ffn_layer(xp, l, w, p, gf, tiles_per_seq, final_norm=(l == depth - 1))
    out = xp.reshape(n_batch, tiles_per_seq, KSUB, SUBLANES, d)
    out = jnp.transpose(out, (0, 1, 3, 2, 4)).reshape(n_batch, seq, d)
    return out.astype(x.dtype)
```

```python
import functools

import jax
import jax.numpy as jnp
from jax import lax
from jax.experimental import pallas as pl
from jax.experimental.pallas import tpu as pltpu

D_MODEL = 2048
MIX_W = 512
IN_COLS = 6 * MIX_W
S5_GROUP_CH = 16
S5_STATE = 64
CONV_WIDTH = 31
LRU_HEADS = 8
LRU_HEAD_DIM = MIX_W // LRU_HEADS
LRU_CONV_WIDTH = 4
LRU_C = 8.0
POOL_WINDOWS = (2, 4, 8, 16)
FFN_DIM = 5504
FFN_CONV_WIDTH = 3
EPS = 1e-6

SUBLANES = 8
LANES = 128

T = 256
KSUB = T // SUBLANES
FFN_TM = 1024
FFN_TF = 512
FFN_PAD = 5632
N_FCHUNK = FFN_PAD // FFN_TF
S5_BLK = 4
S5_BLK_STATES = 512
S5_COLS = 2 * S5_BLK * S5_BLK_STATES
CONV_ROWS = 32
VMEM_LIMIT = 56 * 1024 * 1024
FFN_VMEM_LIMIT = VMEM_LIMIT

MID_U, MID_X, MID_G = 0, MIX_W, 2 * MIX_W
MIDF_COLS = 3 * MIX_W
MIDB_COLS = 2 * MIX_W

F32 = jnp.float32
BF16 = jnp.bfloat16


def _dot(a, b):
    return jnp.dot(a, b, preferred_element_type=F32)


def _rms(x, g):
    ms = jnp.mean(x * x, axis=-1, keepdims=True)
    return x * lax.rsqrt(ms + EPS) * g


def _cmul(ar, ai, br, bi):
    return ar * br - ai * bi, ar * bi + ai * br


def _sub_roll(x, d):
    return pltpu.roll(x, d, 0)


def _rows(k):
    return slice(SUBLANES * k, SUBLANES * (k + 1))


def _fill_ext(ext_ref, tail_ref, a, dv, first):
    w = a.shape[1]
    n = SUBLANES * dv
    ext_ref[n:n + T, :] = a
    tail = a[T - n:, :].reshape(dv, SUBLANES, w)
    prev = jnp.where(first, 0.0, tail_ref[...]).reshape(dv, SUBLANES, w)
    sub = lax.broadcasted_iota(jnp.int32, (dv, SUBLANES, w), 1)
    halo = pltpu.roll(jnp.where(sub == SUBLANES - 1, prev, tail), 1, 1)
    ext_ref[0:n, :] = halo.reshape(n, w)
    tail_ref[...] = a[T - n:, :]


def _front_kernel(x_ref, g_ref, win_ref, cvw_ref, cvb_ref, lng_ref, lnb_ref, wpw_ref, bpw_ref,
                  lw_ref, lb_ref, pw_ref, ps_ref,
                  midf_ref, midb_ref,
                  proj_s, cvext_s, cvtail_s, hs_s, lext_s, ltail_s, pext_s, ptail_s,
                  *, n_tiles, tiles_per_seq):
    s = pl.program_id(0)
    cv_halo = CONV_WIDTH - 1
    l_halo = LRU_CONV_WIDTH - 1
    p_halo = max(POOL_WINDOWS) - 1
    n_grp = CONV_ROWS // SUBLANES

    def project():
        h = _rms(x_ref[...], g_ref[...]).astype(BF16)
        proj_s[...] = _dot(h, win_ref[...])

    def gather(t):
        first = (t % tiles_per_seq) == 0
        midf_ref[:, MID_U:MID_U + MIX_W] = proj_s[:, 0:MIX_W]
        hc = proj_s[:, MIX_W:2 * MIX_W] * jax.nn.sigmoid(proj_s[:, 2 * MIX_W:3 * MIX_W])
        _fill_ext(cvext_s, cvtail_s, hc, cv_halo, first)
        _fill_ext(lext_s, ltail_s, proj_s[:, 3 * MIX_W:4 * MIX_W], l_halo, first)
        midf_ref[:, MID_G:MID_G + MIX_W] = jax.nn.gelu(proj_s[:, 4 * MIX_W:5 * MIX_W], approximate=True)
        _fill_ext(pext_s, ptail_s, proj_s[:, 5 * MIX_W:6 * MIX_W], p_halo, first)

    def mixers(t):
        cvb = jnp.broadcast_to(cvb_ref[...], (SUBLANES, MIX_W))
        lbb = jnp.broadcast_to(lb_ref[...], (SUBLANES, MIX_W))
        for rb in range(T // CONV_ROWS):
            r0 = rb * CONV_ROWS
            accs = [cvb] * n_grp
            for kk in range(CONV_WIDTH):
                w8 = cvw_ref[_rows(kk), :]
                for gq in range(n_grp):
                    lo = r0 + SUBLANES * (gq + kk)
                    accs[gq] = accs[gq] + w8 * cvext_s[lo:lo + SUBLANES, :]
            acc = jnp.concatenate(accs, axis=0)
            mu = jnp.mean(acc, axis=-1, keepdims=True)
            cen = acc - mu
            var = jnp.mean(cen * cen, axis=-1, keepdims=True)
            hn = cen * lax.rsqrt(var + EPS) * lng_ref[...] + lnb_ref[...]
            hs_s[r0:r0 + CONV_ROWS, :] = (hn * jax.nn.sigmoid(hn)).astype(BF16)
            xcs = [lbb] * n_grp
            for kk in range(LRU_CONV_WIDTH):
                w8 = lw_ref[_rows(kk), :]
                for gq in range(n_grp):
                    lo = r0 + SUBLANES * (gq + kk)
                    xcs[gq] = xcs[gq] + w8 * lext_s[lo:lo + SUBLANES, :]
            midf_ref[r0:r0 + CONV_ROWS, MID_X:MID_X + MIX_W] = jnp.concatenate(xcs, axis=0)
        midb_ref[:, 0:MIX_W] = (_dot(hs_s[...], wpw_ref[...]) + bpw_ref[...]).astype(BF16)

        c = t % tiles_per_seq
        row = lax.broadcasted_iota(jnp.int32, (T, LANES), 0)
        tok = c * T + (row & (SUBLANES - 1)) * KSUB + (row >> 3)
        base = SUBLANES * p_halo
        for gi, win in enumerate(POOL_WINDOWS):
            cs = slice(LANES * gi, LANES * (gi + 1))
            cur = pext_s[base:base + T, cs]
            ssum = cur
            for dd in range(1, win):
                ssum = ssum + pext_s[base - SUBLANES * dd:base - SUBLANES * dd + T, cs]
            cnt = jnp.minimum(tok + 1, win).astype(F32)
            dg = ssum / cnt - cur
            y = _dot(dg.astype(BF16), pw_ref[gi]) * ps_ref[:, cs]
            midb_ref[:, MIX_W + LANES * gi:MIX_W + LANES * (gi + 1)] = y.astype(BF16)

    @pl.when(s == 0)
    def _():
        cvtail_s[...] = jnp.zeros_like(cvtail_s)
        ltail_s[...] = jnp.zeros_like(ltail_s)
        ptail_s[...] = jnp.zeros_like(ptail_s)
        project()

    @pl.when((s > 0) & (s < n_tiles))
    def _():
        gather(s - 1)
        project()
        mixers(s - 1)

    @pl.when(s == n_tiles)
    def _():
        gather(s - 1)
        mixers(s - 1)


def _back_kernel(midf_ref, midb_ref, x_ref, bw_ref, abar_ref, cw_ref, d_ref, wglu_ref, bglu_ref,
                 wr_ref, br_ref, wi_ref, bi_ref, sp_ref, wout_ref, o_ref,
                 st_s, scar_s, ys5_s, rpre_s, ipre_s, la_s, lh_s, lcar_s, mixed_s,
                 *, n_tiles, tiles_per_seq):
    s = pl.program_id(0)

    def out_proj():
        o_ref[...] = (x_ref[...] + _dot(mixed_s[...], wout_ref[0:2 * MIX_W, :])
                      + _dot(midb_ref[...], wout_ref[2 * MIX_W:4 * MIX_W, :]))

    def mixer_dots():
        for m in range(S5_BLK):
            u = midf_ref[:, MID_U + LANES * m:MID_U + LANES * (m + 1)]
            st_s[:, 1024 * m:1024 * (m + 1)] = _dot(u.astype(BF16), bw_ref[m])
        for hh in range(2):
            cs = slice(256 * hh, 256 * (hh + 1))
            xcb = midf_ref[:, MID_X + 256 * hh:MID_X + 256 * (hh + 1)].astype(BF16)
            rpre_s[:, cs] = _dot(xcb, wr_ref[hh]) + br_ref[:, cs]
            ipre_s[:, cs] = _dot(xcb, wi_ref[hh]) + bi_ref[:, cs]

    def mixer_scans(t):
        first = (t % tiles_per_seq) == 0
        sub = lax.broadcasted_iota(jnp.int32, (SUBLANES, MIX_W), 0)
        keep = jnp.logical_not(first)

        for m in range(S5_BLK):
            rc = slice(1024 * m, 1024 * m + S5_BLK_STATES)
            ic = slice(1024 * m + S5_BLK_STATES, 1024 * (m + 1))
            a_re, a_im = abar_ref[:, rc], abar_ref[:, ic]
            fr = jnp.zeros((SUBLANES, S5_BLK_STATES), F32)
            fi = fr
            for k in range(KSUB):
                tr, ti = _cmul(a_re, a_im, fr, fi)
                fr = tr + st_s[_rows(k), rc]
                fi = ti + st_s[_rows(k), ic]
            pk_re, pk_im = a_re, a_im
            for _ in range(KSUB.bit_length() - 1):
                pk_re, pk_im = _cmul(pk_re, pk_im, pk_re, pk_im)
            xr = jnp.where(sub == 0, jnp.where(keep, _sub_roll(scar_s[:, rc], 1), 0.0), _sub_roll(fr, 1))
            xi = jnp.where(sub == 0, jnp.where(keep, _sub_roll(scar_s[:, ic], 1), 0.0), _sub_roll(fi, 1))
            qr, qi = pk_re, pk_im
            for dlt in (1, 2, 4):
                shr = jnp.where(sub >= dlt, _sub_roll(xr, dlt), 0.0)
                shi = jnp.where(sub >= dlt, _sub_roll(xi, dlt), 0.0)
                tr, ti = _cmul(qr, qi, shr, shi)
                xr, xi = xr + tr, xi + ti
                qr, qi = _cmul(qr, qi, qr, qi)
            tr, ti = _cmul(pk_re, pk_im, xr, xi)
            scar_s[:, rc] = fr + tr
            scar_s[:, ic] = fi + ti
            sr, si = xr, xi
            for k in range(KSUB):
                tr, ti = _cmul(a_re, a_im, sr, si)
                sr = tr + st_s[_rows(k), rc]
                si = ti + st_s[_rows(k), ic]
                st_s[_rows(k), rc] = sr
                st_s[_rows(k), ic] = si
        for m in range(S5_BLK):
            cs = slice(LANES * m, LANES * (m + 1))
            y = _dot(st_s[:, 1024 * m:1024 * (m + 1)].astype(BF16), cw_ref[m])
            ys5_s[:, cs] = y + d_ref[:, cs] * midf_ref[:, MID_U + LANES * m:MID_U + LANES * (m + 1)]
        gl = jax.nn.gelu(ys5_s[...], approximate=True)
        gate = jax.nn.sigmoid(_dot(gl.astype(BF16), wglu_ref[...]) + bglu_ref[...])
        y_s5 = (gl * gate).astype(BF16)

        spb = jnp.broadcast_to(sp_ref[...], (SUBLANES, MIX_W))
        p = jnp.ones((SUBLANES, MIX_W), F32)
        hst = jnp.zeros((SUBLANES, MIX_W), F32)
        for k in range(KSUB):
            r = jax.nn.sigmoid(rpre_s[_rows(k), :])
            ig = jax.nn.sigmoid(ipre_s[_rows(k), :])
            log_a = -LRU_C * r * spb
            a = jnp.exp(log_a)
            th = jnp.tanh(log_a)
            b = jnp.sqrt(-2.0 * th / (1.0 - th)) * (ig * midf_ref[_rows(k), MID_X:MID_X + MIX_W])
            p = a * p
            hst = a * hst + b
            la_s[_rows(k), :] = a
            lh_s[_rows(k), :] = b
        xin = jnp.where(sub == 0, jnp.where(keep, _sub_roll(lcar_s[...], 1), 0.0), _sub_roll(hst, 1))
        mul = _sub_roll(p, 1)
        for dlt in (1, 2, 4):
            xs = jnp.where(sub >= dlt, _sub_roll(xin, dlt), 0.0)
            ms = jnp.where(sub >= dlt, _sub_roll(mul, dlt), 1.0)
            xin = xin + mul * xs
            mul = mul * ms
        lcar_s[...] = hst + p * xin
        hst = xin
        for k in range(KSUB):
            hst = la_s[_rows(k), :] * hst + lh_s[_rows(k), :]
            lh_s[_rows(k), :] = hst * midf_ref[_rows(k), MID_G:MID_G + MIX_W]

        mixed_s[:, 0:MIX_W] = y_s5
        mixed_s[:, MIX_W:2 * MIX_W] = lh_s[...].astype(BF16)

    @pl.when(s == 0)
    def _():
        scar_s[...] = jnp.zeros_like(scar_s)
        lcar_s[...] = jnp.zeros_like(lcar_s)
        mixer_dots()
        mixer_scans(s)

    @pl.when((s > 0) & (s < n_tiles))
    def _():
        mixer_dots()
        out_proj()
        mixer_scans(s)

    @pl.when(s == n_tiles)
    def _():
        out_proj()


def _ffn_kernel(x_ref, g_ref, wg_ref, wv_ref, cw_ref, cb_ref, wd_ref, gf_ref, o_ref,
                h_s, gate0_s, gate1_s, val0_s, val1_s, act_s, tail_s, *, final_norm, tiles_per_seq):
    i = pl.program_id(0)
    j = pl.program_id(1)
    n_sub = FFN_TM // T
    pad = 2 * SUBLANES
    gates = (gate0_s, gate1_s)
    vals = (val0_s, val1_s)

    def up_proj(par):
        hb = h_s[...]
        gates[par][pad:pad + FFN_TM, :] = _dot(hb, wg_ref[...])
        vals[par][...] = _dot(hb, wv_ref[...]).astype(BF16)

    def down_proj(par, jj):
        g_s = gates[par]
        v_s = vals[par]
        w0 = cw_ref[0:1, :]
        w1 = cw_ref[1:2, :]
        w2 = cw_ref[2:3, :]
        cb = cb_ref[...]
        gc = (cb + w0 * g_s[0:FFN_TM, :] + w1 * g_s[SUBLANES:SUBLANES + FFN_TM, :]
              + w2 * g_s[pad:pad + FFN_TM, :])
        act_s[...] = (jax.nn.gelu(gc, approximate=True) * v_s[...].astype(F32)).astype(BF16)
        sub = lax.broadcasted_iota(jnp.int32, (2, SUBLANES, FFN_TF), 1)
        for s in range(n_sub):
            lo = pad + s * T
            tail = g_s[lo + T - pad:lo + T, :].reshape(2, SUBLANES, FFN_TF)
            prev = tail_s[jj] if s == 0 else g_s[lo - pad:lo, :]
            first = ((i * n_sub + s) % tiles_per_seq) == 0
            prev = jnp.where(first, 0.0, prev).reshape(2, SUBLANES, FFN_TF)
            halo = pltpu.roll(jnp.where(sub == SUBLANES - 1, prev, tail), 1, 1)
            g0 = g_s[lo:lo + SUBLANES, :]
            g1 = g_s[lo + SUBLANES:lo + pad, :]
            fix = jnp.concatenate([cb + w0 * halo[0] + w1 * halo[1] + w2 * g0,
                                   cb + w0 * halo[1] + w1 * g0 + w2 * g1], axis=0)
            act_s[s * T:s * T + pad, :] = (
                jax.nn.gelu(fix, approximate=True) * v_s[s * T:s * T + pad, :].astype(F32)).astype(BF16)
        tail_s[jj] = g_s[FFN_TM:FFN_TM + pad, :]
        ncol = 512
        for n in range(D_MODEL // ncol):
            cs = slice(ncol * n, ncol * (n + 1))
            o_ref[:, cs] = o_ref[:, cs] + _dot(act_s[...], wd_ref[:, cs])

    @pl.when((i == 0) & (j == 0))
    def _():
        tail_s[...] = jnp.zeros_like(tail_s)
        gate0_s[0:pad, :] = jnp.zeros((pad, FFN_TF), F32)
        gate1_s[0:pad, :] = jnp.zeros((pad, FFN_TF), F32)

    @pl.when(j == 0)
    def _():
        x = x_ref[...]
        h_s[...] = _rms(x, g_ref[...]).astype(BF16)
        o_ref[...] = x
        up_proj(0)

    for par in range(2):
        @pl.when((j > 0) & (j < N_FCHUNK) & (j % 2 == par))
        def _(par=par):
            up_proj(par)
            down_proj(1 - par, j - 1)

    @pl.when(j == N_FCHUNK)
    def _():
        down_proj((N_FCHUNK - 1) % 2, N_FCHUNK - 1)
        if final_norm:
            o_ref[...] = _rms(o_ref[...], gf_ref[...])


def _const_spec(shape):
    nd = len(shape)
    return pl.BlockSpec(shape, lambda *_: (0,) * nd, pipeline_mode=pl.Buffered(1))


def _layer_spec(l, shape):
    nd = len(shape)
    return pl.BlockSpec((None,) + tuple(shape), lambda *_: (l,) + (0,) * nd,
                        pipeline_mode=pl.Buffered(1))


def _front_layer(xp, l, w, p, tiles_per_seq):
    n_rows = xp.shape[0]
    n_tiles = n_rows // T
    small = [p["cv_w"], p["cv_b"], p["ln_g"], p["ln_b"], p["w_pw"], p["b_pw"],
             p["l_w"], p["l_b"], p["pool_w"], p["pool_s"]]
    cur = lambda s: (jnp.minimum(s, n_tiles - 1), 0)
    prv = lambda s: (jnp.maximum(s - 1, 0), 0)
    cv_halo = SUBLANES * (CONV_WIDTH - 1)
    l_halo = SUBLANES * (LRU_CONV_WIDTH - 1)
    p_halo = SUBLANES * (max(POOL_WINDOWS) - 1)
    scratch = [
        pltpu.VMEM((T, IN_COLS), F32),
        pltpu.VMEM((cv_halo + T, MIX_W), F32),
        pltpu.VMEM((cv_halo, MIX_W), F32),
        pltpu.VMEM((T, MIX_W), BF16),
        pltpu.VMEM((l_halo + T, MIX_W), F32),
        pltpu.VMEM((l_halo, MIX_W), F32),
        pltpu.VMEM((p_halo + T, MIX_W), F32),
        pltpu.VMEM((p_halo, MIX_W), F32),
    ]
    kern = functools.partial(_front_kernel, n_tiles=n_tiles, tiles_per_seq=tiles_per_seq)
    return pl.pallas_call(
        kern,
        grid=(n_tiles + 1,),
        in_specs=[pl.BlockSpec((T, D_MODEL), cur), _const_spec(p["g"].shape),
                  _layer_spec(l, (D_MODEL, IN_COLS))] + [_const_spec(a.shape) for a in small],
        out_specs=[pl.BlockSpec((T, MIDF_COLS), prv), pl.BlockSpec((T, MIDB_COLS), prv)],
        out_shape=[jax.ShapeDtypeStruct((n_rows, MIDF_COLS), F32),
                   jax.ShapeDtypeStruct((n_rows, MIDB_COLS), BF16)],
        scratch_shapes=scratch,
        compiler_params=pltpu.CompilerParams(
            dimension_semantics=("arbitrary",), vmem_limit_bytes=VMEM_LIMIT),
        name="front_layer",
    )(xp, p["g"], w["w_in"], *small)


def _back_layer(xp, midf, midb, l, w, p, tiles_per_seq):
    n_rows = xp.shape[0]
    n_tiles = n_rows // T
    small = [p["bw"], p["abar"], p["cw"], p["d"], p["w_glu"], p["b_glu"],
             p["w_r"], p["b_r"], p["w_i"], p["b_i"], p["sp"]]
    cur = lambda s: (jnp.minimum(s, n_tiles - 1), 0)
    prv = lambda s: (jnp.maximum(s - 1, 0), 0)
    scratch = [
        pltpu.VMEM((T, S5_COLS), F32),
        pltpu.VMEM((SUBLANES, S5_COLS), F32),
        pltpu.VMEM((T, MIX_W), F32),
        pltpu.VMEM((T, MIX_W), F32),
        pltpu.VMEM((T, MIX_W), F32),
        pltpu.VMEM((T, MIX_W), F32),
        pltpu.VMEM((T, MIX_W), F32),
        pltpu.VMEM((SUBLANES, MIX_W), F32),
        pltpu.VMEM((T, 2 * MIX_W), BF16),
    ]
    kern = functools.partial(_back_kernel, n_tiles=n_tiles, tiles_per_seq=tiles_per_seq)
    return pl.pallas_call(
        kern,
        grid=(n_tiles + 1,),
        in_specs=[pl.BlockSpec((T, MIDF_COLS), cur), pl.BlockSpec((T, MIDB_COLS), prv),
                  pl.BlockSpec((T, D_MODEL), prv)] + [_const_spec(a.shape) for a in small]
                 + [_layer_spec(l, (D_MODEL, D_MODEL))],
        out_specs=pl.BlockSpec((T, D_MODEL), prv),
        out_shape=jax.ShapeDtypeStruct((n_rows, D_MODEL), F32),
        scratch_shapes=scratch,
        compiler_params=pltpu.CompilerParams(
            dimension_semantics=("arbitrary",), vmem_limit_bytes=VMEM_LIMIT),
        name="back_layer",
    )(midf, midb, xp, *small, w["w_out"])


def _ffn_layer(xp, l, w, p, gf, tiles_per_seq, final_norm):
    n_rows = xp.shape[0]
    last = N_FCHUNK - 1
    up_idx = lambda j: jnp.minimum(j, last)
    dn_idx = lambda j: jnp.maximum(j - 1, 0)
    in_specs = [
        pl.BlockSpec((FFN_TM, D_MODEL), lambda i, j: (i, 0), pipeline_mode=pl.Buffered(1)),
        _const_spec((1, D_MODEL)),
        pl.BlockSpec((None, None, D_MODEL, FFN_TF), lambda i, j: (l, up_idx(j), 0, 0)),
        pl.BlockSpec((None, None, D_MODEL, FFN_TF), lambda i, j: (l, N_FCHUNK + up_idx(j), 0, 0)),
        pl.BlockSpec((SUBLANES, FFN_TF), lambda i, j: (0, dn_idx(j))),
        pl.BlockSpec((1, FFN_TF), lambda i, j: (0, dn_idx(j))),
        pl.BlockSpec((None, FFN_TF, D_MODEL), lambda i, j: (l, dn_idx(j), 0)),
        _const_spec((1, D_MODEL)),
    ]
    pad = 2 * SUBLANES
    scratch = [
        pltpu.VMEM((FFN_TM, D_MODEL), BF16),
        pltpu.VMEM((pad + FFN_TM, FFN_TF), F32),
        pltpu.VMEM((pad + FFN_TM, FFN_TF), F32),
        pltpu.VMEM((FFN_TM, FFN_TF), BF16),
        pltpu.VMEM((FFN_TM, FFN_TF), BF16),
        pltpu.VMEM((FFN_TM, FFN_TF), BF16),
        pltpu.VMEM((N_FCHUNK, pad, FFN_TF), F32),
    ]
    kern = functools.partial(_ffn_kernel, final_norm=final_norm, tiles_per_seq=tiles_per_seq)
    return pl.pallas_call(
        kern,
        grid=(n_rows // FFN_TM, N_FCHUNK + 1),
        in_specs=in_specs,
        out_specs=pl.BlockSpec((FFN_TM, D_MODEL), lambda i, j: (i, 0)),
        out_shape=jax.ShapeDtypeStruct((n_rows, D_MODEL), F32),
        scratch_shapes=scratch,
        compiler_params=pltpu.CompilerParams(
            dimension_semantics=("arbitrary", "arbitrary"), vmem_limit_bytes=FFN_VMEM_LIMIT),
        name="ffn_final" if final_norm else "ffn_layer",
    )(xp, p["g_ffn"], w["w_up"], w["w_up"], p["f_cw"], p["f_cb"], w["w_down"], gf)


def _state_cols(z):
    zr = jnp.real(z).astype(F32).reshape(S5_BLK, S5_BLK_STATES)
    zi = jnp.imag(z).astype(F32).reshape(S5_BLK, S5_BLK_STATES)
    return jnp.stack([zr, zi], axis=1).reshape(-1)


def _block_diag(w):
    n, a, b = w.shape
    eye = jnp.eye(n, dtype=w.dtype)
    return jnp.einsum("nab,nm->namb", w, eye).reshape(n * a, n * b)


def _cast_kernel(x_ref, o_ref, *, valid_fn):
    rows_ok, cols_ok = valid_fn(pl.program_id(1))
    x = x_ref[...].reshape(o_ref.shape)
    if rows_ok is not None:
        x = jnp.where(lax.broadcasted_iota(jnp.int32, x.shape, 0) < rows_ok, x, 0.0)
    if cols_ok is not None:
        x = jnp.where(lax.broadcasted_iota(jnp.int32, x.shape, 1) < cols_ok, x, 0.0)
    o_ref[...] = x.astype(BF16)


def _cast_weight(x, out_shape, in_block, in_index, out_block, n_blocks, valid_fn, name):
    return pl.pallas_call(
        functools.partial(_cast_kernel, valid_fn=valid_fn),
        grid=(x.shape[0], n_blocks),
        in_specs=[pl.BlockSpec(in_block, in_index)],
        out_specs=pl.BlockSpec(out_block[0], out_block[1]),
        out_shape=jax.ShapeDtypeStruct(out_shape, BF16),
        compiler_params=pltpu.CompilerParams(
            dimension_semantics=("arbitrary", "arbitrary"), vmem_limit_bytes=VMEM_LIMIT),
        name=name,
    )(x)


def _prep_big(a):
    w = {}
    depth = a["w_in"].shape[0]
    no_mask = lambda b: (None, None)
    cb = 512
    w["w_in"] = _cast_weight(
        a["w_in"], (depth, D_MODEL, IN_COLS), (None, D_MODEL, cb), lambda l, b: (l, 0, b),
        ((None, D_MODEL, cb), lambda l, b: (l, 0, b)), IN_COLS // cb, no_mask, "cast_w_in")
    swap = lambda b: b + (b == 1).astype(jnp.int32) - (b == 2).astype(jnp.int32)
    w["w_out"] = _cast_weight(
        a["w_out"], (depth, D_MODEL, D_MODEL), (None, MIX_W, D_MODEL), lambda l, b: (l, swap(b), 0),
        ((None, MIX_W, D_MODEL), lambda l, b: (l, b, 0)), D_MODEL // MIX_W, no_mask, "cast_w_out")
    lanes_tf, lanes_f = FFN_TF // LANES, FFN_DIM // LANES
    up_off = lambda b: LANES * jnp.where(b < N_FCHUNK, lanes_tf * b, lanes_f + lanes_tf * (b - N_FCHUNK))
    up_valid = lambda b: (None, jnp.where(b < N_FCHUNK, FFN_DIM, 2 * FFN_DIM) - up_off(b))
    w["w_up"] = _cast_weight(
        a["ffn_w_up"], (depth, 2 * N_FCHUNK, D_MODEL, FFN_TF),
        (pl.Element(1), pl.Element(D_MODEL), pl.Element(FFN_TF)), lambda l, b: (l, 0, up_off(b)),
        ((None, None, D_MODEL, FFN_TF), lambda l, b: (l, b, 0, 0)), 2 * N_FCHUNK, up_valid, "cast_w_up")
    w["w_down"] = _cast_weight(
        a["ffn_w_down"], (depth, FFN_PAD, D_MODEL), (None, FFN_TF, D_MODEL), lambda l, b: (l, b, 0),
        ((None, FFN_TF, D_MODEL), lambda l, b: (l, b, 0)), N_FCHUNK,
        lambda b: (FFN_DIM - FFN_TF * b, None), "cast_w_down")
    return w


def _prep_layer(l, a):
    row = lambda v: v[l].reshape(1, -1).astype(F32)
    p = {}
    p["g"] = row(a["norm_mix_g"])
    lam = lax.complex(a["s5_lam_re"][l].astype(F32), a["s5_lam_im"][l].astype(F32))
    step = jnp.exp(a["s5_log_step"][l].astype(F32))[:, None]
    lam_bar = jnp.exp(lam * step)
    bmat = lax.complex(a["s5_b_re"][l].astype(F32), a["s5_b_im"][l].astype(F32))
    b_bar = ((lam_bar - 1.0) / lam)[..., None] * bmat
    bt = jnp.transpose(b_bar, (0, 2, 1))
    bre = jnp.real(bt).reshape(S5_BLK, 8, S5_GROUP_CH, S5_STATE)
    bim = jnp.imag(bt).reshape(S5_BLK, 8, S5_GROUP_CH, S5_STATE)
    bw = jnp.concatenate([jax.vmap(_block_diag)(bre), jax.vmap(_block_diag)(bim)], axis=-1)
    p["bw"] = bw.astype(BF16)
    p["abar"] = jnp.tile(_state_cols(lam_bar)[None, :], (SUBLANES, 1))
    cre = jnp.transpose(a["s5_c_re"][l].astype(F32), (0, 2, 1)).reshape(S5_BLK, 8, S5_STATE, S5_GROUP_CH)
    cim = jnp.transpose(a["s5_c_im"][l].astype(F32), (0, 2, 1)).reshape(S5_BLK, 8, S5_STATE, S5_GROUP_CH)
    cw = jnp.concatenate([jax.vmap(_block_diag)(cre), -jax.vmap(_block_diag)(cim)], axis=1)
    p["cw"] = cw.astype(BF16)
    p["d"] = row(a["s5_d"])
    p["w_glu"] = a["s5_w_glu"][l].astype(BF16)
    p["b_glu"] = row(a["s5_b_glu"])
    p["cv_w"] = jnp.repeat(a["cv_w_dw"][l].astype(F32), SUBLANES, axis=0)
    p["cv_b"] = row(a["cv_b_dw"])
    p["ln_g"] = row(a["cv_ln_g"])
    p["ln_b"] = row(a["cv_ln_b"])
    p["w_pw"] = a["cv_w_pw"][l].astype(BF16)
    p["b_pw"] = row(a["cv_b_pw"])
    p["l_w"] = jnp.repeat(a["lru_w_conv"][l].astype(F32), SUBLANES, axis=0)
    p["l_b"] = row(a["lru_b_conv"])
    p["w_r"] = jax.vmap(_block_diag)(a["lru_w_r"][l].reshape(2, 4, LRU_HEAD_DIM, LRU_HEAD_DIM)).astype(BF16)
    p["b_r"] = row(a["lru_b_r"])
    p["w_i"] = jax.vmap(_block_diag)(a["lru_w_i"][l].reshape(2, 4, LRU_HEAD_DIM, LRU_HEAD_DIM)).astype(BF16)
    p["b_i"] = row(a["lru_b_i"])
    p["sp"] = jax.nn.softplus(-row(a["lru_lam"]))
    p["pool_w"] = a["pool_w"][l].astype(BF16)
    p["pool_s"] = row(a["pool_scale"])
    p["g_ffn"] = row(a["norm_ffn_g"])
    fpad = FFN_PAD - FFN_DIM
    p["f_cw"] = jnp.pad(a["ffn_w_dw"][l].astype(F32), ((0, SUBLANES - FFN_CONV_WIDTH), (0, fpad)))
    p["f_cb"] = jnp.pad(row(a["ffn_b_dw"]), ((0, 0), (0, fpad)))
    return p


def kernel(x, norm_mix_g, w_in, s5_lam_re, s5_lam_im, s5_log_step, s5_b_re, s5_b_im, s5_c_re, s5_c_im, s5_d, s5_w_glu, s5_b_glu, cv_w_dw, cv_b_dw, cv_ln_g, cv_ln_b, cv_w_pw, cv_b_pw, lru_w_conv, lru_b_conv, lru_w_r, lru_b_r, lru_w_i, lru_b_i, lru_lam, pool_w, pool_scale, w_out, norm_ffn_g, ffn_w_up, ffn_w_dw, ffn_b_dw, ffn_w_down, norm_final_g):
    a = dict(norm_mix_g=norm_mix_g, w_in=w_in, s5_lam_re=s5_lam_re, s5_lam_im=s5_lam_im,
             s5_log_step=s5_log_step, s5_b_re=s5_b_re, s5_b_im=s5_b_im, s5_c_re=s5_c_re,
             s5_c_im=s5_c_im, s5_d=s5_d, s5_w_glu=s5_w_glu, s5_b_glu=s5_b_glu, cv_w_dw=cv_w_dw,
             cv_b_dw=cv_b_dw, cv_ln_g=cv_ln_g, cv_ln_b=cv_ln_b, cv_w_pw=cv_w_pw, cv_b_pw=cv_b_pw,
             lru_w_conv=lru_w_conv, lru_b_conv=lru_b_conv, lru_w_r=lru_w_r, lru_b_r=lru_b_r,
             lru_w_i=lru_w_i, lru_b_i=lru_b_i, lru_lam=lru_lam, pool_w=pool_w,
             pool_scale=pool_scale, w_out=w_out, norm_ffn_g=norm_ffn_g, ffn_w_up=ffn_w_up,
             ffn_w_dw=ffn_w_dw, ffn_b_dw=ffn_b_dw, ffn_w_down=ffn_w_down)
    n_batch, seq, d = x.shape
    depth = w_in.shape[0]
    assert d == D_MODEL and seq % T == 0 and (n_batch * seq) % FFN_TM == 0 and seq % FFN_TM == 0
    tiles_per_seq = seq // T
    xp = x.astype(F32).reshape(n_batch, tiles_per_seq, SUBLANES, KSUB, d)
    xp = jnp.transpose(xp, (0, 1, 3, 2, 4)).reshape(n_batch * seq, d)
    gf = norm_final_g.reshape(1, -1).astype(F32)
    w = _prep_big(a)
    for l in range(depth):
        p = _prep_layer(l, a)
        midf, midb = _front_layer(xp, l, w, p, tiles_per_seq)
        xp = _back_layer(xp, midf, midb, l, w, p, tiles_per_seq)
        xp = _ffn_layer(xp, l, w, p, gf, tiles_per_seq, final_norm=(l == depth - 1))
    out = xp.reshape(n_batch, tiles_per_seq, KSUB, SUBLANES, d)
    out = jnp.transpose(out, (0, 1, 3, 2, 4)).reshape(n_batch, seq, d)
    return out.astype(x.dtype)
```

```python
import functools

import jax
import jax.numpy as jnp
from jax import lax
from jax.experimental import pallas as pl
from jax.experimental.pallas import tpu as pltpu

D_MODEL = 2048
MIX_W = 512
IN_COLS = 6 * MIX_W
S5_GROUP_CH = 16
S5_STATE = 64
CONV_WIDTH = 31
LRU_HEADS = 8
LRU_HEAD_DIM = MIX_W // LRU_HEADS
LRU_CONV_WIDTH = 4
LRU_C = 8.0
POOL_WINDOWS = (2, 4, 8, 16)
FFN_DIM = 5504
FFN_CONV_WIDTH = 3
EPS = 1e-6

SUBLANES = 8
LANES = 128

T = 512
KSUB = T // SUBLANES
FFN_TM = 1024
FFN_TF = 512
FFN_PAD = 5632
N_FCHUNK = FFN_PAD // FFN_TF
S5_BLK = 4
S5_BLK_STATES = 512
S5_COLS = 2 * S5_BLK * S5_BLK_STATES
CONV_ROWS = 32
VMEM_LIMIT = 56 * 1024 * 1024
FFN_VMEM_LIMIT = VMEM_LIMIT

MID_U, MID_X, MID_G = 0, MIX_W, 2 * MIX_W
MIDF_COLS = 3 * MIX_W
MIDB_COLS = 2 * MIX_W

F32 = jnp.float32
BF16 = jnp.bfloat16


def _dot(a, b):
    return jnp.dot(a, b, preferred_element_type=F32)


def _rms(x, g):
    ms = jnp.mean(x * x, axis=-1, keepdims=True)
    return x * lax.rsqrt(ms + EPS) * g


def _cmul(ar, ai, br, bi):
    return ar * br - ai * bi, ar * bi + ai * br


def _sub_roll(x, d):
    return pltpu.roll(x, d, 0)


def _rows(k):
    return slice(SUBLANES * k, SUBLANES * (k + 1))


def _fill_ext(ext_ref, tail_ref, a, dv, first):
    w = a.shape[1]
    n = SUBLANES * dv
    ext_ref[n:n + T, :] = a
    tail = a[T - n:, :].reshape(dv, SUBLANES, w)
    prev = jnp.where(first, 0.0, tail_ref[...]).reshape(dv, SUBLANES, w)
    sub = lax.broadcasted_iota(jnp.int32, (dv, SUBLANES, w), 1)
    halo = pltpu.roll(jnp.where(sub == SUBLANES - 1, prev, tail), 1, 1)
    ext_ref[0:n, :] = halo.reshape(n, w)
    tail_ref[...] = a[T - n:, :]


def _front_kernel(x_ref, g_ref, win_ref, cvw_ref, cvb_ref, lng_ref, lnb_ref, wpw_ref, bpw_ref,
                  lw_ref, lb_ref, pw_ref, ps_ref,
                  midf_ref, midb_ref,
                  proj_s, cvext_s, cvtail_s, hs_s, lext_s, ltail_s, pext_s, ptail_s,
                  *, n_tiles, tiles_per_seq):
    s = pl.program_id(0)
    cv_halo = CONV_WIDTH - 1
    l_halo = LRU_CONV_WIDTH - 1
    p_halo = max(POOL_WINDOWS) - 1
    n_grp = CONV_ROWS // SUBLANES

    def project():
        h = _rms(x_ref[...], g_ref[...]).astype(BF16)
        proj_s[...] = _dot(h, win_ref[...])

    def gather(t):
        first = (t % tiles_per_seq) == 0
        midf_ref[:, MID_U:MID_U + MIX_W] = proj_s[:, 0:MIX_W]
        hc = proj_s[:, MIX_W:2 * MIX_W] * jax.nn.sigmoid(proj_s[:, 2 * MIX_W:3 * MIX_W])
        _fill_ext(cvext_s, cvtail_s, hc, cv_halo, first)
        _fill_ext(lext_s, ltail_s, proj_s[:, 3 * MIX_W:4 * MIX_W], l_halo, first)
        midf_ref[:, MID_G:MID_G + MIX_W] = jax.nn.gelu(proj_s[:, 4 * MIX_W:5 * MIX_W], approximate=True)
        _fill_ext(pext_s, ptail_s, proj_s[:, 5 * MIX_W:6 * MIX_W], p_halo, first)

    def mixers(t):
        cvb = jnp.broadcast_to(cvb_ref[...], (SUBLANES, MIX_W))
        lbb = jnp.broadcast_to(lb_ref[...], (SUBLANES, MIX_W))
        for rb in range(T // CONV_ROWS):
            r0 = rb * CONV_ROWS
            accs = [cvb] * n_grp
            for kk in range(CONV_WIDTH):
                w8 = cvw_ref[_rows(kk), :]
                for gq in range(n_grp):
                    lo = r0 + SUBLANES * (gq + kk)
                    accs[gq] = accs[gq] + w8 * cvext_s[lo:lo + SUBLANES, :]
            acc = jnp.concatenate(accs, axis=0)
            mu = jnp.mean(acc, axis=-1, keepdims=True)
            cen = acc - mu
            var = jnp.mean(cen * cen, axis=-1, keepdims=True)
            hn = cen * lax.rsqrt(var + EPS) * lng_ref[...] + lnb_ref[...]
            hs_s[r0:r0 + CONV_ROWS, :] = (hn * jax.nn.sigmoid(hn)).astype(BF16)
            xcs = [lbb] * n_grp
            for kk in range(LRU_CONV_WIDTH):
                w8 = lw_ref[_rows(kk), :]
                for gq in range(n_grp):
                    lo = r0 + SUBLANES * (gq + kk)
                    xcs[gq] = xcs[gq] + w8 * lext_s[lo:lo + SUBLANES, :]
            midf_ref[r0:r0 + CONV_ROWS, MID_X:MID_X + MIX_W] = jnp.concatenate(xcs, axis=0)
        midb_ref[:, 0:MIX_W] = (_dot(hs_s[...], wpw_ref[...]) + bpw_ref[...]).astype(BF16)

        c = t % tiles_per_seq
        row = lax.broadcasted_iota(jnp.int32, (T, LANES), 0)
        tok = c * T + (row & (SUBLANES - 1)) * KSUB + (row >> 3)
        base = SUBLANES * p_halo
        for gi, win in enumerate(POOL_WINDOWS):
            cs = slice(LANES * gi, LANES * (gi + 1))
            cur = pext_s[base:base + T, cs]
            ssum = cur
            for dd in range(1, win):
                ssum = ssum + pext_s[base - SUBLANES * dd:base - SUBLANES * dd + T, cs]
            cnt = jnp.minimum(tok + 1, win).astype(F32)
            dg = ssum / cnt - cur
            y = _dot(dg.astype(BF16), pw_ref[gi]) * ps_ref[:, cs]
            midb_ref[:, MIX_W + LANES * gi:MIX_W + LANES * (gi + 1)] = y.astype(BF16)

    @pl.when(s == 0)
    def _():
        cvtail_s[...] = jnp.zeros_like(cvtail_s)
        ltail_s[...] = jnp.zeros_like(ltail_s)
        ptail_s[...] = jnp.zeros_like(ptail_s)
        project()

    @pl.when((s > 0) & (s < n_tiles))
    def _():
        gather(s - 1)
        project()
        mixers(s - 1)

    @pl.when(s == n_tiles)
    def _():
        gather(s - 1)
        mixers(s - 1)


def _back_kernel(midf_ref, midb_ref, x_ref, bw_ref, abar_ref, cw_ref, d_ref, wglu_ref, bglu_ref,
                 wr_ref, br_ref, wi_ref, bi_ref, sp_ref, wout_ref, o_ref,
                 st_s, scar_s, ys5_s, rpre_s, ipre_s, la_s, lh_s, lcar_s, mixed_s,
                 *, n_tiles, tiles_per_seq):
    s = pl.program_id(0)

    def out_proj():
        o_ref[...] = (x_ref[...] + _dot(mixed_s[...], wout_ref[0:2 * MIX_W, :])
                      + _dot(midb_ref[...], wout_ref[2 * MIX_W:4 * MIX_W, :]))

    def mixer_dots():
        for m in range(S5_BLK):
            u = midf_ref[:, MID_U + LANES * m:MID_U + LANES * (m + 1)]
            st_s[:, 1024 * m:1024 * (m + 1)] = _dot(u.astype(BF16), bw_ref[m])
        for hh in range(2):
            cs = slice(256 * hh, 256 * (hh + 1))
            xcb = midf_ref[:, MID_X + 256 * hh:MID_X + 256 * (hh + 1)].astype(BF16)
            rpre_s[:, cs] = _dot(xcb, wr_ref[hh]) + br_ref[:, cs]
            ipre_s[:, cs] = _dot(xcb, wi_ref[hh]) + bi_ref[:, cs]

    def mixer_scans(t):
        first = (t % tiles_per_seq) == 0
        sub = lax.broadcasted_iota(jnp.int32, (SUBLANES, MIX_W), 0)
        keep = jnp.logical_not(first)

        for m in range(S5_BLK):
            rc = slice(1024 * m, 1024 * m + S5_BLK_STATES)
            ic = slice(1024 * m + S5_BLK_STATES, 1024 * (m + 1))
            a_re, a_im = abar_ref[:, rc], abar_ref[:, ic]
            fr = jnp.zeros((SUBLANES, S5_BLK_STATES), F32)
            fi = fr
            for k in range(KSUB):
                tr, ti = _cmul(a_re, a_im, fr, fi)
                fr = tr + st_s[_rows(k), rc]
                fi = ti + st_s[_rows(k), ic]
            pk_re, pk_im = a_re, a_im
            for _ in range(KSUB.bit_length() - 1):
                pk_re, pk_im = _cmul(pk_re, pk_im, pk_re, pk_im)
            xr = jnp.where(sub == 0, jnp.where(keep, _sub_roll(scar_s[:, rc], 1), 0.0), _sub_roll(fr, 1))
            xi = jnp.where(sub == 0, jnp.where(keep, _sub_roll(scar_s[:, ic], 1), 0.0), _sub_roll(fi, 1))
            qr, qi = pk_re, pk_im
            for dlt in (1, 2, 4):
                shr = jnp.where(sub >= dlt, _sub_roll(xr, dlt), 0.0)
                shi = jnp.where(sub >= dlt, _sub_roll(xi, dlt), 0.0)
                tr, ti = _cmul(qr, qi, shr, shi)
                xr, xi = xr + tr, xi + ti
                qr, qi = _cmul(qr, qi, qr, qi)
            tr, ti = _cmul(pk_re, pk_im, xr, xi)
            scar_s[:, rc] = fr + tr
            scar_s[:, ic] = fi + ti
            sr, si = xr, xi
            for k in range(KSUB):
                tr, ti = _cmul(a_re, a_im, sr, si)
                sr = tr + st_s[_rows(k), rc]
                si = ti + st_s[_rows(k), ic]
                st_s[_rows(k), rc] = sr
                st_s[_rows(k), ic] = si
        for m in range(S5_BLK):
            cs = slice(LANES * m, LANES * (m + 1))
            y = _dot(st_s[:, 1024 * m:1024 * (m + 1)].astype(BF16), cw_ref[m])
            ys5_s[:, cs] = y + d_ref[:, cs] * midf_ref[:, MID_U + LANES * m:MID_U + LANES * (m + 1)]
        gl = jax.nn.gelu(ys5_s[...], approximate=True)
        gate = jax.nn.sigmoid(_dot(gl.astype(BF16), wglu_ref[...]) + bglu_ref[...])
        y_s5 = (gl * gate).astype(BF16)

        spb = jnp.broadcast_to(sp_ref[...], (SUBLANES, MIX_W))
        p = jnp.ones((SUBLANES, MIX_W), F32)
        hst = jnp.zeros((SUBLANES, MIX_W), F32)
        for k in range(KSUB):
            r = jax.nn.sigmoid(rpre_s[_rows(k), :])
            ig = jax.nn.sigmoid(ipre_s[_rows(k), :])
            log_a = -LRU_C * r * spb
            a = jnp.exp(log_a)
            th = jnp.tanh(log_a)
            b = jnp.sqrt(-2.0 * th / (1.0 - th)) * (ig * midf_ref[_rows(k), MID_X:MID_X + MIX_W])
            p = a * p
            hst = a * hst + b
            la_s[_rows(k), :] = a
            lh_s[_rows(k), :] = b
        xin = jnp.where(sub == 0, jnp.where(keep, _sub_roll(lcar_s[...], 1), 0.0), _sub_roll(hst, 1))
        mul = _sub_roll(p, 1)
        for dlt in (1, 2, 4):
            xs = jnp.where(sub >= dlt, _sub_roll(xin, dlt), 0.0)
            ms = jnp.where(sub >= dlt, _sub_roll(mul, dlt), 1.0)
            xin = xin + mul * xs
            mul = mul * ms
        lcar_s[...] = hst + p * xin
        hst = xin
        for k in range(KSUB):
            hst = la_s[_rows(k), :] * hst + lh_s[_rows(k), :]
            lh_s[_rows(k), :] = hst * midf_ref[_rows(k), MID_G:MID_G + MIX_W]

        mixed_s[:, 0:MIX_W] = y_s5
        mixed_s[:, MIX_W:2 * MIX_W] = lh_s[...].astype(BF16)

    @pl.when(s == 0)
    def _():
        scar_s[...] = jnp.zeros_like(scar_s)
        lcar_s[...] = jnp.zeros_like(lcar_s)
        mixer_dots()
        mixer_scans(s)

    @pl.when((s > 0) & (s < n_tiles))
    def _():
        mixer_dots()
        out_proj()
        mixer_scans(s)

    @pl.when(s == n_tiles)
    def _():
        out_proj()


def _ffn_kernel(x_hbm, g_ref, wg_ref, wv_ref, cw_ref, cb_ref, wd_ref, gf_ref, o_ref,
                h_s, gate0_s, gate1_s, val0_s, val1_s, act_s, tail_s, x_s, x_sem,
                *, final_norm, tiles_per_seq, n_blocks):
    i = pl.program_id(0)
    j = pl.program_id(1)
    n_sub = FFN_TM // T
    pad = 2 * SUBLANES
    gates = (gate0_s, gate1_s)
    vals = (val0_s, val1_s)

    def up_proj(par):
        hb = h_s[...]
        gates[par][pad:pad + FFN_TM, :] = _dot(hb, wg_ref[...])
        vals[par][...] = _dot(hb, wv_ref[...])

    def down_proj(par, jj):
        g_s = gates[par]
        v_s = vals[par]
        w0 = cw_ref[0:1, :]
        w1 = cw_ref[1:2, :]
        w2 = cw_ref[2:3, :]
        cb = cb_ref[...]
        gc = (cb + w0 * g_s[0:FFN_TM, :] + w1 * g_s[SUBLANES:SUBLANES + FFN_TM, :]
              + w2 * g_s[pad:pad + FFN_TM, :])
        act_s[...] = (jax.nn.gelu(gc, approximate=True) * v_s[...]).astype(BF16)
        sub = lax.broadcasted_iota(jnp.int32, (2, SUBLANES, FFN_TF), 1)
        for s in range(n_sub):
            lo = pad + s * T
            tail = g_s[lo + T - pad:lo + T, :].reshape(2, SUBLANES, FFN_TF)
            prev = tail_s[jj] if s == 0 else g_s[lo - pad:lo, :]
            first = ((i * n_sub + s) % tiles_per_seq) == 0
            prev = jnp.where(first, 0.0, prev).reshape(2, SUBLANES, FFN_TF)
            halo = pltpu.roll(jnp.where(sub == SUBLANES - 1, prev, tail), 1, 1)
            g0 = g_s[lo:lo + SUBLANES, :]
            g1 = g_s[lo + SUBLANES:lo + pad, :]
            fix = jnp.concatenate([cb + w0 * halo[0] + w1 * halo[1] + w2 * g0,
                                   cb + w0 * halo[1] + w1 * g0 + w2 * g1], axis=0)
            act_s[s * T:s * T + pad, :] = (
                jax.nn.gelu(fix, approximate=True) * v_s[s * T:s * T + pad, :]).astype(BF16)
        tail_s[jj] = g_s[FFN_TM:FFN_TM + pad, :]
        ncol = 512
        for n in range(D_MODEL // ncol):
            cs = slice(ncol * n, ncol * (n + 1))
            o_ref[:, cs] = o_ref[:, cs] + _dot(act_s[...], wd_ref[:, cs])

    def x_copy(block):
        rows = pl.ds(pl.multiple_of(block * FFN_TM, FFN_TM), FFN_TM)
        return pltpu.make_async_copy(x_hbm.at[rows, :], x_s, x_sem.at[0])

    @pl.when((i == 0) & (j == 0))
    def _():
        x_copy(0).start()
        tail_s[...] = jnp.zeros_like(tail_s)
        gate0_s[0:pad, :] = jnp.zeros((pad, FFN_TF), F32)
        gate1_s[0:pad, :] = jnp.zeros((pad, FFN_TF), F32)

    @pl.when(j == 0)
    def _():
        x_copy(i).wait()
        x = x_s[...]
        h_s[...] = _rms(x, g_ref[...]).astype(BF16)
        o_ref[...] = x
        up_proj(0)

    @pl.when((j == 1) & (i + 1 < n_blocks))
    def _():
        x_copy(i + 1).start()

    for par in range(2):
        @pl.when((j > 0) & (j < N_FCHUNK) & (j % 2 == par))
        def _(par=par):
            up_proj(par)
            down_proj(1 - par, j - 1)

    @pl.when(j == N_FCHUNK)
    def _():
        down_proj((N_FCHUNK - 1) % 2, N_FCHUNK - 1)
        if final_norm:
            o_ref[...] = _rms(o_ref[...], gf_ref[...])


def _const_spec(shape):
    nd = len(shape)
    return pl.BlockSpec(shape, lambda *_: (0,) * nd, pipeline_mode=pl.Buffered(1))


def _layer_spec(l, shape):
    nd = len(shape)
    return pl.BlockSpec((None,) + tuple(shape), lambda *_: (l,) + (0,) * nd,
                        pipeline_mode=pl.Buffered(1))


def _front_layer(xp, l, w, p, tiles_per_seq):
    n_rows = xp.shape[0]
    n_tiles = n_rows // T
    small = [p["cv_w"], p["cv_b"], p["ln_g"], p["ln_b"], p["w_pw"], p["b_pw"],
             p["l_w"], p["l_b"], p["pool_w"], p["pool_s"]]
    cur = lambda s: (jnp.minimum(s, n_tiles - 1), 0)
    prv = lambda s: (jnp.maximum(s - 1, 0), 0)
    cv_halo = SUBLANES * (CONV_WIDTH - 1)
    l_halo = SUBLANES * (LRU_CONV_WIDTH - 1)
    p_halo = SUBLANES * (max(POOL_WINDOWS) - 1)
    scratch = [
        pltpu.VMEM((T, IN_COLS), F32),
        pltpu.VMEM((cv_halo + T, MIX_W), F32),
        pltpu.VMEM((cv_halo, MIX_W), F32),
        pltpu.VMEM((T, MIX_W), BF16),
        pltpu.VMEM((l_halo + T, MIX_W), F32),
        pltpu.VMEM((l_halo, MIX_W), F32),
        pltpu.VMEM((p_halo + T, MIX_W), F32),
        pltpu.VMEM((p_halo, MIX_W), F32),
    ]
    kern = functools.partial(_front_kernel, n_tiles=n_tiles, tiles_per_seq=tiles_per_seq)
    return pl.pallas_call(
        kern,
        grid=(n_tiles + 1,),
        in_specs=[pl.BlockSpec((T, D_MODEL), cur), _const_spec(p["g"].shape),
                  _layer_spec(l, (D_MODEL, IN_COLS))] + [_const_spec(a.shape) for a in small],
        out_specs=[pl.BlockSpec((T, MIDF_COLS), prv), pl.BlockSpec((T, MIDB_COLS), prv)],
        out_shape=[jax.ShapeDtypeStruct((n_rows, MIDF_COLS), F32),
                   jax.ShapeDtypeStruct((n_rows, MIDB_COLS), BF16)],
        scratch_shapes=scratch,
        compiler_params=pltpu.CompilerParams(
            dimension_semantics=("arbitrary",), vmem_limit_bytes=VMEM_LIMIT),
        name="front_layer",
    )(xp, p["g"], w["w_in"], *small)


def _back_layer(xp, midf, midb, l, w, p, tiles_per_seq):
    n_rows = xp.shape[0]
    n_tiles = n_rows // T
    small = [p["bw"], p["abar"], p["cw"], p["d"], p["w_glu"], p["b_glu"],
             p["w_r"], p["b_r"], p["w_i"], p["b_i"], p["sp"]]
    cur = lambda s: (jnp.minimum(s, n_tiles - 1), 0)
    prv = lambda s: (jnp.maximum(s - 1, 0), 0)
    scratch = [
        pltpu.VMEM((T, S5_COLS), F32),
        pltpu.VMEM((SUBLANES, S5_COLS), F32),
        pltpu.VMEM((T, MIX_W), F32),
        pltpu.VMEM((T, MIX_W), F32),
        pltpu.VMEM((T, MIX_W), F32),
        pltpu.VMEM((T, MIX_W), F32),
        pltpu.VMEM((T, MIX_W), F32),
        pltpu.VMEM((SUBLANES, MIX_W), F32),
        pltpu.VMEM((T, 2 * MIX_W), BF16),
    ]
    kern = functools.partial(_back_kernel, n_tiles=n_tiles, tiles_per_seq=tiles_per_seq)
    return pl.pallas_call(
        kern,
        grid=(n_tiles + 1,),
        in_specs=[pl.BlockSpec((T, MIDF_COLS), cur), pl.BlockSpec((T, MIDB_COLS), prv),
                  pl.BlockSpec((T, D_MODEL), prv)] + [_const_spec(a.shape) for a in small]
                 + [_layer_spec(l, (D_MODEL, D_MODEL))],
        out_specs=pl.BlockSpec((T, D_MODEL), prv),
        out_shape=jax.ShapeDtypeStruct((n_rows, D_MODEL), F32),
        scratch_shapes=scratch,
        compiler_params=pltpu.CompilerParams(
            dimension_semantics=("arbitrary",), vmem_limit_bytes=VMEM_LIMIT),
        name="back_layer",
    )(midf, midb, xp, *small, w["w_out"])


def _ffn_layer(xp, l, w, p, gf, tiles_per_seq, final_norm):
    n_rows = xp.shape[0]
    last = N_FCHUNK - 1
    up_idx = lambda j: jnp.minimum(j, last)
    dn_idx = lambda j: jnp.maximum(j - 1, 0)
    in_specs = [
        pl.BlockSpec(memory_space=pl.ANY),
        _const_spec((1, D_MODEL)),
        pl.BlockSpec((None, D_MODEL, FFN_TF), lambda i, j: (l, 0, up_idx(j))),
        pl.BlockSpec((None, D_MODEL, FFN_TF), lambda i, j: (l, 0, N_FCHUNK + up_idx(j))),
        pl.BlockSpec((SUBLANES, FFN_TF), lambda i, j: (0, dn_idx(j))),
        pl.BlockSpec((1, FFN_TF), lambda i, j: (0, dn_idx(j))),
        pl.BlockSpec((None, FFN_TF, D_MODEL), lambda i, j: (l, dn_idx(j), 0)),
        _const_spec((1, D_MODEL)),
    ]
    pad = 2 * SUBLANES
    scratch = [
        pltpu.VMEM((FFN_TM, D_MODEL), BF16),
        pltpu.VMEM((pad + FFN_TM, FFN_TF), F32),
        pltpu.VMEM((pad + FFN_TM, FFN_TF), F32),
        pltpu.VMEM((FFN_TM, FFN_TF), F32),
        pltpu.VMEM((FFN_TM, FFN_TF), F32),
        pltpu.VMEM((FFN_TM, FFN_TF), BF16),
        pltpu.VMEM((N_FCHUNK, pad, FFN_TF), F32),
        pltpu.VMEM((FFN_TM, D_MODEL), F32),
        pltpu.SemaphoreType.DMA((1,)),
    ]
    kern = functools.partial(_ffn_kernel, final_norm=final_norm, tiles_per_seq=tiles_per_seq,
                             n_blocks=n_rows // FFN_TM)
    return pl.pallas_call(
        kern,
        grid=(n_rows // FFN_TM, N_FCHUNK + 1),
        in_specs=in_specs,
        out_specs=pl.BlockSpec((FFN_TM, D_MODEL), lambda i, j: (i, 0)),
        out_shape=jax.ShapeDtypeStruct((n_rows, D_MODEL), F32),
        scratch_shapes=scratch,
        compiler_params=pltpu.CompilerParams(
            dimension_semantics=("arbitrary", "arbitrary"), vmem_limit_bytes=FFN_VMEM_LIMIT),
        name="ffn_final" if final_norm else "ffn_layer",
    )(xp, p["g_ffn"], w["w_up"], w["w_up"], p["f_cw"], p["f_cb"], w["w_down"], gf)


def _state_cols(z):
    zr = jnp.real(z).astype(F32).reshape(S5_BLK, S5_BLK_STATES)
    zi = jnp.imag(z).astype(F32).reshape(S5_BLK, S5_BLK_STATES)
    return jnp.stack([zr, zi], axis=1).reshape(-1)


def _block_diag(w):
    n, a, b = w.shape
    eye = jnp.eye(n, dtype=w.dtype)
    return jnp.einsum("nab,nm->namb", w, eye).reshape(n * a, n * b)


def _cast_kernel(x_ref, o_ref, *, valid_fn):
    rows_ok, cols_ok, shifted = valid_fn(pl.program_id(1))
    x = x_ref[...].reshape(o_ref.shape)
    if shifted is not None:
        moved = jnp.concatenate([x[:, LANES:], jnp.zeros((x.shape[0], LANES), x.dtype)], axis=1)
        x = jnp.where(shifted, moved, x)
    if rows_ok is not None:
        x = jnp.where(lax.broadcasted_iota(jnp.int32, x.shape, 0) < rows_ok, x, 0.0)
    if cols_ok is not None:
        x = jnp.where(lax.broadcasted_iota(jnp.int32, x.shape, 1) < cols_ok, x, 0.0)
    o_ref[...] = x.astype(BF16)


def _cast_weight(x, out_shape, in_block, in_index, out_block, n_blocks, valid_fn, name):
    return pl.pallas_call(
        functools.partial(_cast_kernel, valid_fn=valid_fn),
        grid=(x.shape[0], n_blocks),
        in_specs=[pl.BlockSpec(in_block, in_index)],
        out_specs=pl.BlockSpec(out_block[0], out_block[1]),
        out_shape=jax.ShapeDtypeStruct(out_shape, BF16),
        compiler_params=pltpu.CompilerParams(
            dimension_semantics=("arbitrary", "arbitrary"), vmem_limit_bytes=VMEM_LIMIT),
        name=name,
    )(x)


def _prep_big(a):
    w = {}
    depth = a["w_in"].shape[0]
    no_mask = lambda b: (None, None, None)
    cb = 512
    w["w_in"] = _cast_weight(
        a["w_in"], (depth, D_MODEL, IN_COLS), (None, D_MODEL, cb), lambda l, b: (l, 0, b),
        ((None, D_MODEL, cb), lambda l, b: (l, 0, b)), IN_COLS // cb, no_mask, "cast_w_in")
    swap = lambda b: b + (b == 1).astype(jnp.int32) - (b == 2).astype(jnp.int32)
    w["w_out"] = _cast_weight(
        a["w_out"], (depth, D_MODEL, D_MODEL), (None, MIX_W, D_MODEL), lambda l, b: (l, swap(b), 0),
        ((None, MIX_W, D_MODEL), lambda l, b: (l, b, 0)), D_MODEL // MIX_W, no_mask, "cast_w_out")
    lanes_tf, lanes_f = FFN_TF // LANES, FFN_DIM // LANES
    up_want = lambda b: jnp.where(b < N_FCHUNK, lanes_tf * b, lanes_f + lanes_tf * (b - N_FCHUNK))
    up_off = lambda b: LANES * jnp.minimum(up_want(b), 2 * lanes_f - lanes_tf)
    up_valid = lambda b: (None, jnp.where(b < N_FCHUNK, FFN_DIM, 2 * FFN_DIM) - LANES * up_want(b),
                          up_want(b) > 2 * lanes_f - lanes_tf)
    assert 2 * lanes_f - lanes_tf == lanes_f + lanes_tf * (N_FCHUNK - 1) - 1
    w["w_up"] = _cast_weight(
        a["ffn_w_up"], (depth, D_MODEL, 2 * FFN_PAD),
        (pl.Element(1), pl.Element(D_MODEL), pl.Element(FFN_TF)), lambda l, b: (l, 0, up_off(b)),
        ((None, D_MODEL, FFN_TF), lambda l, b: (l, 0, b)), 2 * N_FCHUNK, up_valid, "cast_w_up")
    w["w_down"] = _cast_weight(
        a["ffn_w_down"], (depth, FFN_PAD, D_MODEL), (None, FFN_TF, D_MODEL), lambda l, b: (l, b, 0),
        ((None, FFN_TF, D_MODEL), lambda l, b: (l, b, 0)), N_FCHUNK,
        lambda b: (FFN_DIM - FFN_TF * b, None, None), "cast_w_down")
    return w


def _prep_layer(l, a):
    row = lambda v: v[l].reshape(1, -1).astype(F32)
    p = {}
    p["g"] = row(a["norm_mix_g"])
    lam = lax.complex(a["s5_lam_re"][l].astype(F32), a["s5_lam_im"][l].astype(F32))
    step = jnp.exp(a["s5_log_step"][l].astype(F32))[:, None]
    lam_bar = jnp.exp(lam * step)
    bmat = lax.complex(a["s5_b_re"][l].astype(F32), a["s5_b_im"][l].astype(F32))
    b_bar = ((lam_bar - 1.0) / lam)[..., None] * bmat
    bt = jnp.transpose(b_bar, (0, 2, 1))
    bre = jnp.real(bt).reshape(S5_BLK, 8, S5_GROUP_CH, S5_STATE)
    bim = jnp.imag(bt).reshape(S5_BLK, 8, S5_GROUP_CH, S5_STATE)
    bw = jnp.concatenate([jax.vmap(_block_diag)(bre), jax.vmap(_block_diag)(bim)], axis=-1)
    p["bw"] = bw.astype(BF16)
    p["abar"] = jnp.tile(_state_cols(lam_bar)[None, :], (SUBLANES, 1))
    cre = jnp.transpose(a["s5_c_re"][l].astype(F32), (0, 2, 1)).reshape(S5_BLK, 8, S5_STATE, S5_GROUP_CH)
    cim = jnp.transpose(a["s5_c_im"][l].astype(F32), (0, 2, 1)).reshape(S5_BLK, 8, S5_STATE, S5_GROUP_CH)
    cw = jnp.concatenate([jax.vmap(_block_diag)(cre), -jax.vmap(_block_diag)(cim)], axis=1)
    p["cw"] = cw.astype(BF16)
    p["d"] = row(a["s5_d"])
    p["w_glu"] = a["s5_w_glu"][l].astype(BF16)
    p["b_glu"] = row(a["s5_b_glu"])
    p["cv_w"] = jnp.repeat(a["cv_w_dw"][l].astype(F32), SUBLANES, axis=0)
    p["cv_b"] = row(a["cv_b_dw"])
    p["ln_g"] = row(a["cv_ln_g"])
    p["ln_b"] = row(a["cv_ln_b"])
    p["w_pw"] = a["cv_w_pw"][l].astype(BF16)
    p["b_pw"] = row(a["cv_b_pw"])
    p["l_w"] = jnp.repeat(a["lru_w_conv"][l].astype(F32), SUBLANES, axis=0)
    p["l_b"] = row(a["lru_b_conv"])
    p["w_r"] = jax.vmap(_block_diag)(a["lru_w_r"][l].reshape(2, 4, LRU_HEAD_DIM, LRU_HEAD_DIM)).astype(BF16)
    p["b_r"] = row(a["lru_b_r"])
    p["w_i"] = jax.vmap(_block_diag)(a["lru_w_i"][l].reshape(2, 4, LRU_HEAD_DIM, LRU_HEAD_DIM)).astype(BF16)
    p["b_i"] = row(a["lru_b_i"])
    p["sp"] = jax.nn.softplus(-row(a["lru_lam"]))
    p["pool_w"] = a["pool_w"][l].astype(BF16)
    p["pool_s"] = row(a["pool_scale"])
    p["g_ffn"] = row(a["norm_ffn_g"])
    fpad = FFN_PAD - FFN_DIM
    p["f_cw"] = jnp.pad(a["ffn_w_dw"][l].astype(F32), ((0, SUBLANES - FFN_CONV_WIDTH), (0, fpad)))
    p["f_cb"] = jnp.pad(row(a["ffn_b_dw"]), ((0, 0), (0, fpad)))
    return p


def kernel(x, norm_mix_g, w_in, s5_lam_re, s5_lam_im, s5_log_step, s5_b_re, s5_b_im, s5_c_re, s5_c_im, s5_d, s5_w_glu, s5_b_glu, cv_w_dw, cv_b_dw, cv_ln_g, cv_ln_b, cv_w_pw, cv_b_pw, lru_w_conv, lru_b_conv, lru_w_r, lru_b_r, lru_w_i, lru_b_i, lru_lam, pool_w, pool_scale, w_out, norm_ffn_g, ffn_w_up, ffn_w_dw, ffn_b_dw, ffn_w_down, norm_final_g):
    a = dict(norm_mix_g=norm_mix_g, w_in=w_in, s5_lam_re=s5_lam_re, s5_lam_im=s5_lam_im,
             s5_log_step=s5_log_step, s5_b_re=s5_b_re, s5_b_im=s5_b_im, s5_c_re=s5_c_re,
             s5_c_im=s5_c_im, s5_d=s5_d, s5_w_glu=s5_w_glu, s5_b_glu=s5_b_glu, cv_w_dw=cv_w_dw,
             cv_b_dw=cv_b_dw, cv_ln_g=cv_ln_g, cv_ln_b=cv_ln_b, cv_w_pw=cv_w_pw, cv_b_pw=cv_b_pw,
             lru_w_conv=lru_w_conv, lru_b_conv=lru_b_conv, lru_w_r=lru_w_r, lru_b_r=lru_b_r,
             lru_w_i=lru_w_i, lru_b_i=lru_b_i, lru_lam=lru_lam, pool_w=pool_w,
             pool_scale=pool_scale, w_out=w_out, norm_ffn_g=norm_ffn_g, ffn_w_up=ffn_w_up,
             ffn_w_dw=ffn_w_dw, ffn_b_dw=ffn_b_dw, ffn_w_down=ffn_w_down)
    n_batch, seq, d = x.shape
    depth = w_in.shape[0]
    assert d == D_MODEL and seq % T == 0 and (n_batch * seq) % FFN_TM == 0 and seq % FFN_TM == 0
    tiles_per_seq = seq // T
    xp = x.astype(F32).reshape(n_batch, tiles_per_seq, SUBLANES, KSUB, d)
    xp = jnp.transpose(xp, (0, 1, 3, 2, 4)).reshape(n_batch * seq, d)
    gf = norm_final_g.reshape(1, -1).astype(F32)
    w = _prep_big(a)
    for l in range(depth):
        p = _prep_layer(l, a)
        midf, midb = _front_layer(xp, l, w, p, tiles_per_seq)
        xp = _back_layer(xp, midf, midb, l, w, p, tiles_per_seq)
        xp = _ffn_layer(xp, l, w, p, gf, tiles_per_seq, final_norm=(l == depth - 1))
    out = xp.reshape(n_batch, tiles_per_seq, KSUB, SUBLANES, d)
    out = jnp.transpose(out, (0, 1, 3, 2, 4)).reshape(n_batch, seq, d)
    return out.astype(x.dtype)
```
